```python
import jax, jax.numpy as jnp
from jax import lax
import numpy as np


D_MODEL = 1024
BATCH = 4
SEQ = 8192
DEPTH = 2

GRID_W = 64
CTX_LEN = 256
N_EVEN = (DEPTH + 1) // 2
N_ODD = DEPTH // 2
RMS_EPS = 1e-6

POOL_WIDTH = D_MODEL // 2
POOL_WINDOWS = (2, 4, 8, 16)
POOL_GROUP = POOL_WIDTH // len(POOL_WINDOWS)
HG_WIDTH = D_MODEL // 2
HG_HEAD_DIM = 128
HG_HEADS = HG_WIDTH // HG_HEAD_DIM
HG_CHUNK = 64
AB_IN = POOL_WIDTH + 5 * HG_WIDTH
AB_MIX = POOL_WIDTH + HG_WIDTH
ATT_HEAD_DIM = 128
ATT_HEADS = D_MODEL // ATT_HEAD_DIM
ATT_KV_HEADS = 2
ATT_GROUP = ATT_HEADS // ATT_KV_HEADS
ATT_IN = (ATT_HEADS + 2 * ATT_KV_HEADS) * ATT_HEAD_DIM
Q_BLOCK = 128
ROPE_THETA = 10000.0
FFN_HIDDEN = ((8 * D_MODEL + 3 * 256 - 1) // (3 * 256)) * 256

kernel_name = "hybrid_pool_hgrn2_gqa_diffusion_trunk"

F32 = jnp.float32


def rmsnorm(x, g):
    xf = x.astype(F32)
    y = xf * lax.rsqrt(jnp.mean(xf * xf, axis=-1, keepdims=True) + RMS_EPS)
    return (y * g.astype(F32)).astype(x.dtype)


def flip(a):
    return a[:, ::-1]


def swiglu(h, w_in, w_out):
    a, b = jnp.split(h @ w_in, 2, axis=-1)
    return (jax.nn.silu(a) * b) @ w_out


def grid_rope(x):
    L = x.shape[1]
    rows = L // GRID_W
    row = jnp.repeat(jnp.arange(rows, dtype=F32), GRID_W)
    col = jnp.tile(jnp.arange(GRID_W, dtype=F32), rows)
    n_freq = ATT_HEAD_DIM // 4
    inv = ROPE_THETA ** (-jnp.arange(n_freq, dtype=F32) / n_freq)

    def rot(xa, pos):
        ang = pos[:, None] * inv[None, :]
        cos = jnp.cos(ang)[:, None, :]
        sin = jnp.sin(ang)[:, None, :]
        x1, x2 = jnp.split(xa, 2, axis=-1)
        return jnp.concatenate([x1 * cos - x2 * sin, x2 * cos + x1 * sin], axis=-1)

    xf = x.astype(F32)
    half = ATT_HEAD_DIM // 2
    return jnp.concatenate([rot(xf[..., :half], row), rot(xf[..., half:], col)], axis=-1).astype(x.dtype)


def multiscale_pool(u, pool_w, pool_scale):
    Bn, Ln, _ = u.shape
    cs = jnp.pad(jnp.cumsum(u.astype(F32), axis=1), ((0, 0), (1, 0), (0, 0)))
    t = jnp.arange(Ln)
    groups = []
    for gi, w in enumerate(POOL_WINDOWS):
        lo = jnp.clip(t - w // 2, 0, Ln)
        hi = jnp.clip(t - w // 2 + w, 0, Ln)
        sl = slice(gi * POOL_GROUP, (gi + 1) * POOL_GROUP)
        csg = cs[..., sl]
        mean = (csg[:, hi] - csg[:, lo]) / (hi - lo).astype(F32)[None, :, None]
        groups.append(mean - u[..., sl].astype(F32))
    y = jnp.stack(groups, axis=2).astype(u.dtype)
    y = jnp.einsum('blgc,gcd->blgd', y, pool_w).reshape(Bn, Ln, POOL_WIDTH)
    return y * pool_scale


def hgrn_scan(q, k, v, logf, s0):
    Bn, Ln, H, _ = q.shape
    dv = v.shape[-1]
    n = Ln // HG_CHUNK

    def chunks(a):
        return jnp.moveaxis(a.reshape(Bn, n, HG_CHUNK, H, a.shape[-1]), 1, 0)

    causal = jnp.tril(jnp.ones((HG_CHUNK, HG_CHUNK), dtype=bool))[None, :, :, None, None]

    def step(S, inp):
        qc, kc, vc, gc = inp
        b = jnp.cumsum(gc, axis=1)
        dec = jnp.exp(jnp.where(causal, b[:, :, None] - b[:, None, :], -jnp.inf))
        a = jnp.einsum('bthd,btshd->btsh', qc, dec * kc[:, None])
        o = jnp.einsum('btsh,bshe->bthe', a, vc) + jnp.einsum('bthd,bhde->bthe', qc * jnp.exp(b), S)
        b_last = b[:, -1]
        S = jnp.exp(b_last)[..., None] * S + jnp.einsum('bshd,bshe->bhde', kc * jnp.exp(b_last[:, None] - b), vc)
        return S, o

    s_fin, o = lax.scan(step, s0, (chunks(q), chunks(k), chunks(v), chunks(logf)))
    return jnp.moveaxis(o, 0, 1).reshape(Bn, Ln, H, dv), s_fin


def pool_hgrn_mixer(h_lat, h_ctx, w_in, w_out, pool_w, pool_scale, lb, onorm_g, need_ctx):
    def project(h):
        z = h @ w_in
        Bn, Ln = z.shape[:2]
        u = z[..., :POOL_WIDTH]
        q, zf, zb, v, g = jnp.split(z[..., POOL_WIDTH:], 5, axis=-1)
        heads = lambda a: a.reshape(Bn, Ln, HG_HEADS, HG_HEAD_DIM).astype(F32)
        return u, heads(jax.nn.silu(q)), heads(zf), heads(zb), heads(v), g

    def gate(z, lb_dir):
        f = lb_dir + (1.0 - lb_dir) * jax.nn.sigmoid(z)
        return 1.0 - f, jnp.log(f)

    lb_f = lb[0].reshape(HG_HEADS, HG_HEAD_DIM)
    lb_b = lb[1].reshape(HG_HEADS, HG_HEAD_DIM)
    u_c, q_c, zf_c, zb_c, v_c, g_c = project(h_ctx)
    u_l, q_l, zf_l, zb_l, v_l, g_l = project(h_lat)
    k_cf, lf_cf = gate(zf_c, lb_f)
    k_cb, lf_cb = gate(zb_c, lb_b)
    k_lf, lf_lf = gate(zf_l, lb_f)
    k_lb, lf_lb = gate(zb_l, lb_b)

    s0 = jnp.zeros((h_ctx.shape[0], HG_HEADS, HG_HEAD_DIM, HG_HEAD_DIM), F32)
    o_cf, s_f = hgrn_scan(q_c, k_cf, v_c, lf_cf, s0)
    o_cb, s_b = hgrn_scan(flip(q_c), flip(k_cb), flip(v_c), flip(lf_cb), s0)
    o_lf, _ = hgrn_scan(q_l, k_lf, v_l, lf_lf, s_f)
    o_lb, _ = hgrn_scan(flip(q_l), flip(k_lb), flip(v_l), flip(lf_lb), s_b)

    def readout(o_sum, g, u):
        Bn, Ln = g.shape[:2]
        o = rmsnorm(o_sum, onorm_g).reshape(Bn, Ln, HG_WIDTH).astype(g.dtype) * jax.nn.silu(g)
        return jnp.concatenate([multiscale_pool(u, pool_w, pool_scale), o], axis=-1) @ w_out

    y_lat = readout(o_lf + flip(o_lb), g_l, u_l)
    y_ctx = readout(o_cf + flip(o_cb), g_c, u_c) if need_ctx else None
    return y_lat, y_ctx


def gqa_mixer(h_lat, h_ctx, w_in, w_out, qn_g, kn_g, need_ctx):
    dh = ATT_HEAD_DIM

    def project(h):
        z = h @ w_in
        Bn, Ln = z.shape[:2]
        q = z[..., :ATT_HEADS * dh].reshape(Bn, Ln, ATT_HEADS, dh)
        k = z[..., ATT_HEADS * dh:(ATT_HEADS + ATT_KV_HEADS) * dh].reshape(Bn, Ln, ATT_KV_HEADS, dh)
        v = z[..., (ATT_HEADS + ATT_KV_HEADS) * dh:].reshape(Bn, Ln, ATT_KV_HEADS, dh)
        return rmsnorm(q, qn_g), rmsnorm(k, kn_g), v

    Bn, L, _ = h_lat.shape
    Lc = h_ctx.shape[1]
    q_l, k_l, v_l = project(h_lat)
    q_c, k_c, v_c = project(h_ctx)
    q_l = grid_rope(q_l)
    k_l = grid_rope(k_l)
    k_all = jnp.concatenate([k_l, k_c], axis=1)
    v_all = jnp.concatenate([v_l, v_c], axis=1)
    scale = dh ** -0.5

    def attend(qb, k, v):
        s = jnp.einsum('bqhgd,bkhd->bhgqk', qb, k, preferred_element_type=F32) * scale
        p = jax.nn.softmax(s, axis=-1).astype(v.dtype)
        return jnp.einsum('bhgqk,bkhd->bqhgd', p, v)

    q_blocks = jnp.moveaxis(q_l.reshape(Bn, L // Q_BLOCK, Q_BLOCK, ATT_KV_HEADS, ATT_GROUP, dh), 1, 0)
    o = lax.map(lambda qb: attend(qb, k_all, v_all), q_blocks)
    y_lat = jnp.moveaxis(o, 0, 1).reshape(Bn, L, ATT_HEADS * dh) @ w_out
    y_ctx = None
    if need_ctx:
        o_c = attend(q_c.reshape(Bn, Lc, ATT_KV_HEADS, ATT_GROUP, dh), k_c, v_c)
        y_ctx = o_c.reshape(Bn, Lc, ATT_HEADS * dh) @ w_out
    return y_lat, y_ctx


def setup_inputs(seed: int = 0) -> dict:
    key = jax.random.key(seed)
    ks = jax.random.split(key, 20)
    D = D_MODEL
    nrm = lambda k, shape, s: jax.random.normal(k, shape, F32) * s
    return {
        "x": nrm(ks[0], (BATCH, SEQ, D), 1.0),
        "c": nrm(ks[1], (BATCH, D), 1.0),
        "ctx": nrm(ks[2], (BATCH, CTX_LEN, D), 1.0),
        "c_ctx": nrm(ks[3], (D,), 1.0),
        "ada_w": nrm(ks[4], (DEPTH, D, 6 * D), 0.5 * D ** -0.5),
        "ada_b": nrm(ks[5], (DEPTH, 6 * D), 0.02),
        "norm_g": 1.0 + nrm(ks[6], (DEPTH, 4, D), 0.02),
        "ab_w_in": nrm(ks[7], (N_EVEN, D, AB_IN), D ** -0.5),
        "ab_w_out": nrm(ks[8], (N_EVEN, AB_MIX, D), AB_MIX ** -0.5),
        "pool_w": nrm(ks[9], (N_EVEN, len(POOL_WINDOWS), POOL_GROUP, POOL_GROUP), POOL_GROUP ** -0.5),
        "pool_scale": 1.0 + nrm(ks[10], (N_EVEN, POOL_WIDTH), 0.1),
        "hg_lower": nrm(ks[11], (N_EVEN + 1, 2, HG_WIDTH), 0.1),
        "hg_onorm_g": 1.0 + nrm(ks[12], (N_EVEN, HG_HEAD_DIM), 0.02),
        "att_w_in": nrm(ks[13], (N_ODD, D, ATT_IN), D ** -0.5),
        "att_w_out": nrm(ks[14], (N_ODD, ATT_HEADS * ATT_HEAD_DIM, D), (ATT_HEADS * ATT_HEAD_DIM) ** -0.5),
        "att_qnorm_g": 1.0 + nrm(ks[15], (N_ODD, ATT_HEAD_DIM), 0.02),
        "att_knorm_g": 1.0 + nrm(ks[16], (N_ODD, ATT_HEAD_DIM), 0.02),
        "ffn_w_in": nrm(ks[17], (DEPTH, D, 2 * FFN_HIDDEN), D ** -0.5),
        "ffn_w_out": nrm(ks[18], (DEPTH, FFN_HIDDEN, D), FFN_HIDDEN ** -0.5),
    }


def reference(x, c, ctx, c_ctx, ada_w, ada_b, norm_g, ab_w_in, ab_w_out, pool_w, pool_scale,
              hg_lower, hg_onorm_g, att_w_in, att_w_out, att_qnorm_g, att_knorm_g, ffn_w_in, ffn_w_out):
    lb_all = jnp.cumsum(jax.nn.softmax(hg_lower.astype(F32), axis=0), axis=0)
    ctx_s = ctx
    for l in range(DEPTH):
        j = l // 2
        need_ctx = l < DEPTH - 1
        m_lat = (jax.nn.silu(c) @ ada_w[l] + ada_b[l])[:, None]
        m_ctx = jax.nn.silu(c_ctx) @ ada_w[l] + ada_b[l]
        sh1, sc1, g1, sh2, sc2, g2 = jnp.split(m_lat, 6, axis=-1)
        csh1, csc1, cg1, csh2, csc2, cg2 = jnp.split(m_ctx, 6, axis=-1)

        h_lat = rmsnorm(x, norm_g[l, 0]) * (1.0 + sc1) + sh1
        h_ctx = rmsnorm(ctx_s, norm_g[l, 0]) * (1.0 + csc1) + csh1
        if l % 2 == 0:
            y_lat, y_ctx = pool_hgrn_mixer(h_lat, h_ctx, ab_w_in[j], ab_w_out[j], pool_w[j], pool_scale[j],
                                           lb_all[j], hg_onorm_g[j], need_ctx)
        else:
            y_lat, y_ctx = gqa_mixer(h_lat, h_ctx, att_w_in[j], att_w_out[j], att_qnorm_g[j], att_knorm_g[j],
                                     need_ctx)
        x = x + g1 * rmsnorm(y_lat, norm_g[l, 1])
        f_lat = swiglu(rmsnorm(x, norm_g[l, 2]) * (1.0 + sc2) + sh2, ffn_w_in[l], ffn_w_out[l])
        x = x + g2 * rmsnorm(f_lat, norm_g[l, 3])

        if need_ctx:
            ctx_s = ctx_s + cg1 * rmsnorm(y_ctx, norm_g[l, 1])
            f_ctx = swiglu(rmsnorm(ctx_s, norm_g[l, 2]) * (1.0 + csc2) + csh2, ffn_w_in[l], ffn_w_out[l])
            ctx_s = ctx_s + cg2 * rmsnorm(f_ctx, norm_g[l, 3])
    return x
```

```python
import functools

import numpy as np
import jax
import jax.numpy as jnp
from jax import lax
from jax.experimental import pallas as pl
from jax.experimental.pallas import tpu as pltpu

F32 = jnp.float32
BF16 = jnp.bfloat16

RMS_EPS = 1e-6
GRID_W = 64
POOL_WINDOWS = (2, 4, 8, 16)
HEAD_DIM = 128
ATT_KV_HEADS = 2
ROPE_THETA = 10000.0
HG_CHUNK = 64
POOL_HALO = 16
LOG2E = 1.4426950408889634

VMEM_LIMIT = 56 * 1024 * 1024

_NT = (((1,), (1,)), ((), ()))
_TN = (((0,), (0,)), ((), ()))


def _params(sem):
    return pltpu.CompilerParams(dimension_semantics=sem, vmem_limit_bytes=VMEM_LIMIT)


def _rows(tm, w):
    return pl.BlockSpec((None, tm, w), lambda b, i: (b, i, 0))


def _perb(w):
    return pl.BlockSpec((None, 1, w), lambda b, i: (b, 0, 0))


def _full(shape):
    nd = len(shape)
    return pl.BlockSpec(shape, lambda b, i: (0,) * nd, pipeline_mode=pl.Buffered(1))


def _silu(z):
    return z * jax.nn.sigmoid(z)


def _rms(x, g):
    return x * lax.rsqrt(jnp.mean(x * x, axis=-1, keepdims=True) + RMS_EPS) * g


def _row_tile(n):
    return 512 if n % 512 == 0 else 256


def _ada_kernel(c_ref, w_ref, b_ref, o_ref):
    o_ref[...] = jnp.dot(_silu(c_ref[...]), w_ref[...], preferred_element_type=F32) + b_ref[...]


def _ada_call(cc, ada_w, ada_b):
    depth, d, n6 = ada_w.shape
    rows = cc.shape[0]
    tn = n6 // 4
    return pl.pallas_call(
        _ada_kernel,
        grid=(depth, n6 // tn),
        in_specs=[
            pl.BlockSpec((rows, d), lambda l, j: (0, 0)),
            pl.BlockSpec((None, d, tn), lambda l, j: (l, 0, j)),
            pl.BlockSpec((None, 1, tn), lambda l, j: (l, 0, j)),
        ],
        out_specs=pl.BlockSpec((None, rows, tn), lambda l, j: (l, 0, j)),
        out_shape=jax.ShapeDtypeStruct((depth, rows, n6), F32),
        compiler_params=_params(("arbitrary", "arbitrary")),
        name="ada_mod",
    )(cc, ada_w, ada_b.reshape(depth, 1, n6))


def _proj0_kernel(layer_j, pw, hw, x_ref, sc_ref, sh_ref, ng_ref, w_ref, hgl_ref,
                  u_ref, q_ref, kf_ref, kb_ref, lff_ref, lfb_ref, v_ref, g_ref):
    h = (_rms(x_ref[...], ng_ref[...]) * (1.0 + sc_ref[...]) + sh_ref[...]).astype(BF16)

    def col(lo, width):
        return jnp.dot(h, w_ref[:, lo:lo + width], preferred_element_type=F32)

    hgl = hgl_ref[...]
    e = jnp.exp(hgl - jnp.max(hgl, axis=0, keepdims=True))
    sm = e / jnp.sum(e, axis=0, keepdims=True)
    lbound = jnp.sum(sm[:layer_j + 1], axis=0, keepdims=True)

    u_ref[...] = col(0, pw).astype(BF16)
    q_ref[...] = _silu(col(pw, hw)).astype(BF16)
    for d, (k_ref, l_ref) in enumerate(((kf_ref, lff_ref), (kb_ref, lfb_ref))):
        lb = lbound[:, d * hw:(d + 1) * hw]
        f = lb + (1.0 - lb) * jax.nn.sigmoid(col(pw + (1 + d) * hw, hw))
        k_ref[...] = (1.0 - f).astype(BF16)
        l_ref[...] = jnp.log(f)
    v_ref[...] = col(pw + 3 * hw, hw).astype(BF16)
    g_ref[...] = _silu(col(pw + 4 * hw, hw)).astype(BF16)


def _proj0_call(x, sc, sh, ng, w, hgl, layer_j, pw, hw):
    bn, ln, d = x.shape
    tm = _row_tile(ln)
    bf = jax.ShapeDtypeStruct((bn, ln, hw), BF16)
    ff = jax.ShapeDtypeStruct((bn, ln, hw), F32)
    return pl.pallas_call(
        functools.partial(_proj0_kernel, layer_j, pw, hw),
        grid=(bn, ln // tm),
        in_specs=[_rows(tm, d), _perb(d), _perb(d), _full((1, d)), _full(w.shape), _full(hgl.shape)],
        out_specs=[_rows(tm, pw)] + [_rows(tm, hw)] * 7,
        out_shape=[jax.ShapeDtypeStruct((bn, ln, pw), BF16), bf, bf, bf, ff, ff, bf, bf],
        compiler_params=_params(("parallel", "parallel")),
        name="proj0",
    )(x, sc, sh, ng, w, hgl)


def _split3(x):
    hi = x.astype(BF16)
    r = x - hi.astype(F32)
    mid = r.astype(BF16)
    lo = (r - mid.astype(F32)).astype(BF16)
    return hi, mid, lo


def _hg_chunk(q, k, v, b, st, mask, last_row):
    eb = jnp.exp(b)
    q1 = (q.astype(F32) * eb).astype(BF16)
    k1f = k.astype(F32) * jnp.exp(-b)
    ebl = jnp.exp(b[last_row:last_row + 1, :])
    a = lax.dot_general(q1, k1f.astype(BF16), _NT, preferred_element_type=F32)
    a = jnp.where(mask, a, 0.0).astype(BF16)
    o = jnp.dot(a, v, preferred_element_type=F32)
    o = o + lax.dot_general(q1, st.astype(BF16), _NT, preferred_element_type=F32)
    st_new = st * ebl + lax.dot_general(v, (k1f * ebl).astype(BF16), _TN, preferred_element_type=F32)
    return o, st_new


def _hgrn_kernel(nh, qf_ref, kf_ref, lf_ref, vf_ref, qb_ref, kb_ref, lb_ref, vb_ref, s0f_ref, s0b_ref,
                 of_ref, ob_ref, sff_ref, sfb_ref, stf, stb):
    i = pl.program_id(1)
    c = HG_CHUNK
    nch = qf_ref.shape[0] // c

    @pl.when(i == 0)
    def _():
        stf[...] = s0f_ref[...]
        stb[...] = s0b_ref[...]

    r = lax.broadcasted_iota(jnp.int32, (c, c), 0)
    s = lax.broadcasted_iota(jnp.int32, (c, c), 1)
    lower = s <= r
    upper = s >= r
    tri_f = jnp.where(lower, 1.0, 0.0).astype(BF16)
    tri_b = jnp.where(upper, 1.0, 0.0).astype(BF16)

    def cumul(tri, lf):
        hi, mid, lo = _split3(lf)
        return (jnp.dot(tri, hi, preferred_element_type=F32)
                + jnp.dot(tri, mid, preferred_element_type=F32)
                + jnp.dot(tri, lo, preferred_element_type=F32))

    for ci in range(nch):
        for (rev, q_ref, k_ref, l_ref, v_ref, o_ref, st_ref, tri, mask) in (
                (False, qf_ref, kf_ref, lf_ref, vf_ref, of_ref, stf, tri_f, lower),
                (True, qb_ref, kb_ref, lb_ref, vb_ref, ob_ref, stb, tri_b, upper)):
            cc = nch - 1 - ci if rev else ci
            rows = pl.ds(cc * c, c)
            b_all = cumul(tri, l_ref[rows, :])
            for h in range(nh):
                cols = pl.ds(h * HEAD_DIM, HEAD_DIM)
                o, st_new = _hg_chunk(q_ref[rows, cols], k_ref[rows, cols], v_ref[rows, cols],
                                      b_all[:, h * HEAD_DIM:(h + 1) * HEAD_DIM], st_ref[h],
                                      mask, 0 if rev else c - 1)
                st_ref[h] = st_new
                o_ref[rows, cols] = o.astype(BF16)

    @pl.when(i == pl.num_programs(1) - 1)
    def _():
        sff_ref[...] = stf[...]
        sfb_ref[...] = stb[...]


def _hgrn_call(q, kf, kb, lff, lfb, v, s0f, s0b):
    bn, ln, hw = q.shape
    nh = hw // HEAD_DIM
    tb = 256
    n = ln // tb
    fwd = pl.BlockSpec((None, tb, hw), lambda b, i: (b, i, 0))
    bwd = pl.BlockSpec((None, tb, hw), lambda b, i: (b, n - 1 - i, 0))
    st = pl.BlockSpec((None, nh, HEAD_DIM, HEAD_DIM), lambda b, i: (b, 0, 0, 0))
    st_shape = jax.ShapeDtypeStruct((bn, nh, HEAD_DIM, HEAD_DIM), F32)
    o_shape = jax.ShapeDtypeStruct((bn, ln, hw), BF16)
    return pl.pallas_call(
        functools.partial(_hgrn_kernel, nh),
        grid=(bn, n),
        in_specs=[fwd, fwd, fwd, fwd, bwd, bwd, bwd, bwd, st, st],
        out_specs=[fwd, bwd, st, st],
        out_shape=[o_shape, o_shape, st_shape, st_shape],
        scratch_shapes=[pltpu.VMEM((nh, HEAD_DIM, HEAD_DIM), F32)] * 2,
        compiler_params=_params(("parallel", "arbitrary")),
        name="hgrn_scan",
    )(q, kf, lff, v, q, kb, lfb, v, s0f, s0b)


def _mixer_tail(mix, w_ref, x_ref, g1_ref, ng1_ref, ng2_ref, sc2_ref, sh2_ref, x1_ref, h2_ref):
    y = jnp.dot(mix, w_ref[...], preferred_element_type=F32)
    x1 = x_ref[...] + g1_ref[...] * _rms(y, ng1_ref[...])
    x1_ref[...] = x1
    h2_ref[...] = (_rms(x1, ng2_ref[...]) * (1.0 + sc2_ref[...]) + sh2_ref[...]).astype(BF16)


def _mix0_kernel(seq_len, u_ref, up_ref, un_ref, of_ref, ob_ref, g_ref, pw_ref, ps_ref, og_ref, w_ref,
                 x_ref, g1_ref, ng1_ref, ng2_ref, sc2_ref, sh2_ref, x1_ref, h2_ref, ubuf):
    i = pl.program_id(1)
    tm, pwid = u_ref.shape
    hal = POOL_HALO
    pg = pwid // len(POOL_WINDOWS)

    first = i == 0
    last = i == pl.num_programs(1) - 1
    ubuf[0:hal, :] = jnp.where(first, 0.0, up_ref[...].astype(F32))
    ubuf[hal:hal + tm, :] = u_ref[...].astype(F32)
    ubuf[hal + tm:, :] = jnp.where(last, 0.0, un_ref[...].astype(F32))

    t = i * tm + lax.broadcasted_iota(jnp.int32, (tm, pg), 0)
    parts = []
    for gi, w in enumerate(POOL_WINDOWS):
        cols = pl.ds(gi * pg, pg)
        acc = ubuf[pl.ds(hal - w // 2, tm), cols]
        for jj in range(1, w):
            acc = acc + ubuf[pl.ds(hal - w // 2 + jj, tm), cols]
        lo = jnp.maximum(t - w // 2, 0)
        hi = jnp.minimum(t - w // 2 + w, seq_len)
        y = acc / (hi - lo).astype(F32) - ubuf[pl.ds(hal, tm), cols]
        y = jnp.dot(y.astype(BF16), pw_ref[gi], preferred_element_type=F32)
        parts.append((y * ps_ref[:, gi * pg:(gi + 1) * pg]).astype(BF16))

    osum = of_ref[...].astype(F32) + ob_ref[...].astype(F32)
    gate = g_ref[...].astype(F32)
    for h in range(osum.shape[1] // HEAD_DIM):
        sl = slice(h * HEAD_DIM, (h + 1) * HEAD_DIM)
        parts.append((_rms(osum[:, sl], og_ref[...]) * gate[:, sl]).astype(BF16))

    _mixer_tail(jnp.concatenate(parts, axis=-1), w_ref, x_ref, g1_ref, ng1_ref, ng2_ref, sc2_ref, sh2_ref,
                x1_ref, h2_ref)


def _mix0_call(u, of, ob, g, pool_w, pool_scale, og, w_out, x, g1, ng1, ng2, sc2, sh2):
    bn, ln, d = x.shape
    pwid, hw = u.shape[2], g.shape[2]
    tm = _row_tile(ln)
    hb = tm // POOL_HALO
    nhb = ln // POOL_HALO
    prev = pl.BlockSpec((None, POOL_HALO, pwid), lambda b, i: (b, jnp.maximum(i * hb - 1, 0), 0))
    nxt = pl.BlockSpec((None, POOL_HALO, pwid), lambda b, i: (b, jnp.minimum((i + 1) * hb, nhb - 1), 0))
    return pl.pallas_call(
        functools.partial(_mix0_kernel, ln),
        grid=(bn, ln // tm),
        in_specs=[_rows(tm, pwid), prev, nxt, _rows(tm, hw), _rows(tm, hw), _rows(tm, hw),
                  _full(pool_w.shape), _full((1, pwid)), _full((1, HEAD_DIM)), _full(w_out.shape),
                  _rows(tm, d), _perb(d), _full((1, d)), _full((1, d)), _perb(d), _perb(d)],
        out_specs=[_rows(tm, d), _rows(tm, d)],
        out_shape=[jax.ShapeDtypeStruct((bn, ln, d), F32), jax.ShapeDtypeStruct((bn, ln, d), BF16)],
        scratch_shapes=[pltpu.VMEM((tm + 2 * POOL_HALO, pwid), F32)],
        compiler_params=_params(("parallel", "parallel")),
        name="mix0",
    )(u, u, u, of, ob, g, pool_w, pool_scale, og, w_out, x, g1, ng1, ng2, sc2, sh2)


def _ffn_kernel(hc, x1_ref, h2_ref, wi_ref, wo_ref, g2_ref, ng3_ref, o_ref, t_ref):
    hid = wo_ref.shape[0]
    h2 = h2_ref[...]
    for j in range(hid // hc):
        a = jnp.dot(h2, wi_ref[:, j * hc:(j + 1) * hc], preferred_element_type=F32)
        b = jnp.dot(h2, wi_ref[:, hid + j * hc:hid + (j + 1) * hc], preferred_element_type=F32)
        t_ref[:, j * hc:(j + 1) * hc] = (_silu(a) * b).astype(BF16)
    f = jnp.dot(t_ref[...], wo_ref[...], preferred_element_type=F32)
    o_ref[...] = x1_ref[...] + g2_ref[...] * _rms(f, ng3_ref[...])


def _ffn_call(x1, h2, w_in, w_out, g2, ng3):
    bn, ln, d = x1.shape
    hid = w_out.shape[0]
    tm = _row_tile(ln)
    hc = 256
    assert hid % hc == 0
    return pl.pallas_call(
        functools.partial(_ffn_kernel, hc),
        grid=(bn, ln // tm),
        in_specs=[_rows(tm, d), _rows(tm, d), _full(w_in.shape), _full(w_out.shape), _perb(d), _full((1, d))],
        out_specs=_rows(tm, d),
        out_shape=jax.ShapeDtypeStruct((bn, ln, d), F32),
        scratch_shapes=[pltpu.VMEM((tm, hid), BF16)],
        compiler_params=_params(("parallel", "parallel")),
        name="ffn",
    )(x1, h2, w_in, w_out, g2, ng3)


def _rope_tables(ln):
    half = HEAD_DIM // 2
    n_freq = HEAD_DIM // 4
    pos = np.arange(ln)
    inv = (ROPE_THETA ** (-np.arange(n_freq, dtype=np.float32) / n_freq)).astype(np.float32)
    lane = np.arange(HEAD_DIM)
    p = np.where(lane[None, :] < half, (pos // GRID_W)[:, None], (pos % GRID_W)[:, None]).astype(np.float32)
    ang = (p * inv[lane % n_freq][None, :]).astype(np.float32).astype(np.float64)
    first = (lane % half) < n_freq
    cos = np.cos(ang)
    sin = np.sin(ang)
    sin_up = np.where(first[None, :], -sin, 0.0)
    sin_dn = np.where(first[None, :], 0.0, sin)
    return (jnp.asarray(cos, F32), jnp.asarray(sin_up, F32), jnp.asarray(sin_dn, F32))


def _head_norm_rope(z, g, rope, qscale):
    y = _rms(z, g)
    if rope is not None:
        cos, sin_up, sin_dn = rope
        n_freq = HEAD_DIM // 4
        y = (y * cos + pltpu.roll(y, HEAD_DIM - n_freq, 1) * sin_up + pltpu.roll(y, n_freq, 1) * sin_dn)
    return (y * qscale).astype(BF16)


def _proj1_kernel(with_q, with_rope, nq, nkv, *refs):
    x_ref, sc_ref, sh_ref, ng_ref, w_ref, qg_ref, kg_ref = refs[:7]
    refs = refs[7:]
    rope = None
    if with_rope:
        rope = (refs[0][...], refs[1][...], refs[2][...])
        refs = refs[3:]
    h = (_rms(x_ref[...], ng_ref[...]) * (1.0 + sc_ref[...]) + sh_ref[...]).astype(BF16)
    dh = HEAD_DIM

    def col(idx):
        return jnp.dot(h, w_ref[:, idx * dh:(idx + 1) * dh], preferred_element_type=F32)

    if with_q:
        q_ref, k_ref, v_ref = refs
        qscale = dh ** -0.5 * LOG2E
        for hq in range(nq):
            q_ref[:, hq * dh:(hq + 1) * dh] = _head_norm_rope(col(hq), qg_ref[...], rope, qscale)
    else:
        k_ref, v_ref = refs
    for hk in range(nkv):
        k_ref[:, hk * dh:(hk + 1) * dh] = _head_norm_rope(col(nq + hk), kg_ref[...], rope, 1.0)
        v_ref[:, hk * dh:(hk + 1) * dh] = col(nq + nkv + hk).astype(BF16)


def _proj1_call(x, sc, sh, ng, w, qg, kg, with_q, with_rope):
    bn, ln, d = x.shape
    dh = HEAD_DIM
    nkv = ATT_KV_HEADS
    nq = w.shape[1] // dh - 2 * nkv
    tm = _row_tile(ln)
    in_specs = [_rows(tm, d), _perb(d), _perb(d), _full((1, d)), _full(w.shape), _full((1, dh)), _full((1, dh))]
    args = [x, sc, sh, ng, w, qg, kg]
    if with_rope:
        tab = pl.BlockSpec((tm, dh), lambda b, i: (i, 0))
        in_specs += [tab, tab, tab]
        args += list(_rope_tables(ln))
    kv_shape = jax.ShapeDtypeStruct((bn, ln, nkv * dh), BF16)
    out_specs = [_rows(tm, nkv * dh), _rows(tm, nkv * dh)]
    out_shape = [kv_shape, kv_shape]
    if with_q:
        out_specs = [_rows(tm, nq * dh)] + out_specs
        out_shape = [jax.ShapeDtypeStruct((bn, ln, nq * dh), BF16)] + out_shape
    return pl.pallas_call(
        functools.partial(_proj1_kernel, with_q, with_rope, nq, nkv),
        grid=(bn, ln // tm),
        in_specs=in_specs,
        out_specs=out_specs,
        out_shape=out_shape,
        compiler_params=_params(("parallel", "parallel")),
        name="proj1_q" if with_q else "proj1_kv",
    )(*args)


def _attn_kernel(group, tk, q_ref, k_ref, v_ref, o_ref, acc_ref, m_ref):
    dh = HEAD_DIM
    tq = q_ref.shape[0]
    nk = k_ref.shape[0] // tk
    acc_ref[...] = jnp.zeros_like(acc_ref)
    m_ref[...] = jnp.full_like(m_ref, -jnp.inf)
    ones = jnp.ones((tk, dh), BF16)

    def step(j, carry):
        rows = pl.ds(pl.multiple_of(j * tk, tk), tk)
        kc = k_ref[rows, :]
        vc = jnp.concatenate([v_ref[rows, :], ones], axis=-1)
        for g in range(group):
            qg = q_ref[:, g * dh:(g + 1) * dh]
            s = lax.dot_general(qg, kc, _NT, preferred_element_type=F32)
            m_old = m_ref[g]
            m_new = jnp.maximum(m_old, jnp.max(s, axis=1, keepdims=True))
            p = jnp.exp2(s - m_new).astype(BF16)
            alpha = jnp.exp2(m_old - m_new)
            acc_ref[g] = alpha * acc_ref[g] + jnp.dot(p, vc, preferred_element_type=F32)
            m_ref[g] = m_new
        return carry

    lax.fori_loop(0, nk, step, 0)
    for g in range(group):
        acc = acc_ref[g]
        o_ref[:, g * dh:(g + 1) * dh] = (acc[:, :dh] / acc[:, dh:dh + 1]).astype(BF16)


def _attn_call(q, k, v, tq, tk):
    bn, lq, dq = q.shape
    lk = k.shape[1]
    dh = HEAD_DIM
    nkv = k.shape[2] // dh
    group = dq // dh // nkv
    assert lq % tq == 0 and lk % tk == 0
    qspec = pl.BlockSpec((None, tq, group * dh), lambda b, h, i: (b, i, h))
    kvspec = pl.BlockSpec((None, lk, dh), lambda b, h, i: (b, 0, h))
    return pl.pallas_call(
        functools.partial(_attn_kernel, group, tk),
        grid=(bn, nkv, lq // tq),
        in_specs=[qspec, kvspec, kvspec],
        out_specs=qspec,
        out_shape=jax.ShapeDtypeStruct((bn, lq, dq), BF16),
        scratch_shapes=[pltpu.VMEM((group, tq, 2 * dh), F32), pltpu.VMEM((group, tq, 1), F32)],
        compiler_params=_params(("parallel", "parallel", "parallel")),
        name="gqa_attn",
    )(q, k, v)


def _kv_chunk(lk):
    for tk in (768, 512, 384, 256, 128):
        if lk % tk == 0:
            return tk
    raise ValueError(f"key length {lk} is not a multiple of 128")


def _mix1_kernel(o_ref, w_ref, x_ref, g1_ref, ng1_ref, ng2_ref, sc2_ref, sh2_ref, x1_ref, h2_ref):
    _mixer_tail(o_ref[...], w_ref, x_ref, g1_ref, ng1_ref, ng2_ref, sc2_ref, sh2_ref, x1_ref, h2_ref)


def _mix1_call(o, w_out, x, g1, ng1, ng2, sc2, sh2):
    bn, ln, d = x.shape
    tm = _row_tile(ln)
    return pl.pallas_call(
        _mix1_kernel,
        grid=(bn, ln // tm),
        in_specs=[_rows(tm, o.shape[2]), _full(w_out.shape), _rows(tm, d), _perb(d), _full((1, d)),
                  _full((1, d)), _perb(d), _perb(d)],
        out_specs=[_rows(tm, d), _rows(tm, d)],
        out_shape=[jax.ShapeDtypeStruct((bn, ln, d), F32), jax.ShapeDtypeStruct((bn, ln, d), BF16)],
        compiler_params=_params(("parallel", "parallel")),
        name="mix1",
    )(o, w_out, x, g1, ng1, ng2, sc2, sh2)


def kernel(x, c, ctx, c_ctx, ada_w, ada_b, norm_g, ab_w_in, ab_w_out, pool_w, pool_scale, hg_lower, hg_onorm_g,
           att_w_in, att_w_out, att_qnorm_g, att_knorm_g, ffn_w_in, ffn_w_out):
    bn, ln, d = x.shape
    depth = ada_w.shape[0]
    hw = hg_lower.shape[2]
    pwid = pool_scale.shape[1]
    nh = hw // HEAD_DIM

    pad = (-(bn + 1)) % 8
    cc = jnp.concatenate([c, c_ctx[None, :], jnp.zeros((pad, d), F32)], axis=0)
    mods = _ada_call(cc, ada_w, ada_b)

    def split_mods(l, ctx_rows):
        m = mods[l, bn:bn + 1] if ctx_rows else mods[l, :bn]
        m = jnp.broadcast_to(m, (bn, 6 * d)).reshape(bn, 1, 6, d)
        return [m[:, :, t, :] for t in range(6)]

    lat, cx = x, ctx
    for l in range(depth):
        j = l // 2
        need_ctx = l < depth - 1
        ng = [norm_g[l, t][None, :] for t in range(4)]
        m_lat = split_mods(l, False)
        m_ctx = split_mods(l, True)
        w_ffn_in = ffn_w_in[l].astype(BF16)
        w_ffn_out = ffn_w_out[l].astype(BF16)

        if l % 2 == 0:
            w_in = ab_w_in[j].astype(BF16)
            w_out = ab_w_out[j].astype(BF16)
            pw = pool_w[j].astype(BF16)
            ps = pool_scale[j][None, :]
            og = hg_onorm_g[j][None, :]
            hgl = hg_lower.reshape(hg_lower.shape[0], 2 * hw)
            pc = _proj0_call(cx, m_ctx[1], m_ctx[0], ng[0], w_in, hgl, j, pwid, hw)
            plat = _proj0_call(lat, m_lat[1], m_lat[0], ng[0], w_in, hgl, j, pwid, hw)
            zero = jnp.zeros((bn, nh, HEAD_DIM, HEAD_DIM), F32)
            u_c, q_c, kf_c, kb_c, lff_c, lfb_c, v_c, g_c = pc
            u_l, q_l, kf_l, kb_l, lff_l, lfb_l, v_l, g_l = plat
            of_c, ob_c, s_f, s_b = _hgrn_call(q_c, kf_c, kb_c, lff_c, lfb_c, v_c, zero, zero)
            of_l, ob_l, _, _ = _hgrn_call(q_l, kf_l, kb_l, lff_l, lfb_l, v_l, s_f, s_b)
            x1_l, h2_l = _mix0_call(u_l, of_l, ob_l, g_l, pw, ps, og, w_out, lat, m_lat[2], ng[1], ng[2],
                                    m_lat[4], m_lat[3])
            if need_ctx:
                x1_c, h2_c = _mix0_call(u_c, of_c, ob_c, g_c, pw, ps, og, w_out, cx, m_ctx[2], ng[1], ng[2],
                                        m_ctx[4], m_ctx[3])
        else:
            w_in = att_w_in[j].astype(BF16)
            w_out = att_w_out[j].astype(BF16)
            qg = att_qnorm_g[j][None, :]
            kg = att_knorm_g[j][None, :]
            outs_c = _proj1_call(cx, m_ctx[1], m_ctx[0], ng[0], w_in, qg, kg, need_ctx, False)
            q_l, k_l, v_l = _proj1_call(lat, m_lat[1], m_lat[0], ng[0], w_in, qg, kg, True, True)
            k_c, v_c = outs_c[-2], outs_c[-1]
            k_all = jnp.concatenate([k_l, k_c], axis=1)
            v_all = jnp.concatenate([v_l, v_c], axis=1)
            o_l = _attn_call(q_l, k_all, v_all, 256, _kv_chunk(k_all.shape[1]))
            x1_l, h2_l = _mix1_call(o_l, w_out, lat, m_lat[2], ng[1], ng[2], m_lat[4], m_lat[3])
            if need_ctx:
                o_c = _attn_call(outs_c[0], k_c, v_c, 256, _kv_chunk(k_c.shape[1]))
                x1_c, h2_c = _mix1_call(o_c, w_out, cx, m_ctx[2], ng[1], ng[2], m_ctx[4], m_ctx[3])

        lat = _ffn_call(x1_l, h2_l, w_ffn_in, w_ffn_out, m_lat[5], ng[3])
        if need_ctx:
            cx = _ffn_call(x1_c, h2_c, w_ffn_in, w_ffn_out, m_ctx[5], ng[3])
    return lat
```

```python
import functools

import numpy as np
import jax
import jax.numpy as jnp
from jax import lax
from jax.experimental import pallas as pl
from jax.experimental.pallas import tpu as pltpu

F32 = jnp.float32
BF16 = jnp.bfloat16

RMS_EPS = 1e-6
GRID_W = 64
POOL_WINDOWS = (2, 4, 8, 16)
HEAD_DIM = 128
ATT_KV_HEADS = 2
ROPE_THETA = 10000.0
HG_CHUNK = 64
POOL_HALO = 16
LOG2E = 1.4426950408889634
QSCALE = HEAD_DIM ** -0.5 * LOG2E
ONES_ROWS = 16
ATT_SUB = 256
ATT_AHEAD = 3
ATT_MAX_TRIP = 3072
SAFE_BOUND = 60.0

VMEM_LIMIT = 56 * 1024 * 1024

_NT = (((1,), (1,)), ((), ()))
_TN = (((0,), (0,)), ((), ()))


def _params(sem):
    return pltpu.CompilerParams(dimension_semantics=sem, vmem_limit_bytes=VMEM_LIMIT)


def _rows(tm, w):
    return pl.BlockSpec((None, tm, w), lambda b, i: (b, i, 0))


def _perb(w):
    return pl.BlockSpec((None, 1, w), lambda b, i: (b, 0, 0))


def _full(shape):
    nd = len(shape)
    return pl.BlockSpec(shape, lambda b, i: (0,) * nd, pipeline_mode=pl.Buffered(1))


def _silu(z):
    return z * jax.nn.sigmoid(z)


def _rms(x, g):
    return x * lax.rsqrt(jnp.mean(x * x, axis=-1, keepdims=True) + RMS_EPS) * g


def _row_tile(n):
    return 512 if n % 512 == 0 else 256


def _ada_kernel(c_ref, w_ref, b_ref, o_ref):
    o_ref[...] = jnp.dot(_silu(c_ref[...]), w_ref[...], preferred_element_type=F32) + b_ref[...]


def _ada_call(cc, ada_w, ada_b):
    depth, d, n6 = ada_w.shape
    rows = cc.shape[0]
    tn = n6 // 4
    return pl.pallas_call(
        _ada_kernel,
        grid=(depth, n6 // tn),
        in_specs=[
            pl.BlockSpec((rows, d), lambda l, j: (0, 0)),
            pl.BlockSpec((None, d, tn), lambda l, j: (l, 0, j)),
            pl.BlockSpec((None, 1, tn), lambda l, j: (l, 0, j)),
        ],
        out_specs=pl.BlockSpec((None, rows, tn), lambda l, j: (l, 0, j)),
        out_shape=jax.ShapeDtypeStruct((depth, rows, n6), F32),
        compiler_params=_params(("arbitrary", "arbitrary")),
        name="ada_mod",
    )(cc, ada_w, ada_b.reshape(depth, 1, n6))


def _proj0_kernel(layer_j, pw, hw, x_ref, sc_ref, sh_ref, ng_ref, w_ref, hgl_ref,
                  u_ref, q_ref, kf_ref, kb_ref, lff_ref, lfb_ref, v_ref, g_ref):
    h = (_rms(x_ref[...], ng_ref[...]) * (1.0 + sc_ref[...]) + sh_ref[...]).astype(BF16)

    def col(lo, width):
        return jnp.dot(h, w_ref[:, lo:lo + width], preferred_element_type=F32)

    hgl = hgl_ref[...]
    e = jnp.exp(hgl - jnp.max(hgl, axis=0, keepdims=True))
    sm = e / jnp.sum(e, axis=0, keepdims=True)
    lbound = jnp.sum(sm[:layer_j + 1], axis=0, keepdims=True)

    u_ref[...] = col(0, pw).astype(BF16)
    q_ref[...] = _silu(col(pw, hw)).astype(BF16)
    for d, (k_ref, l_ref) in enumerate(((kf_ref, lff_ref), (kb_ref, lfb_ref))):
        lb = lbound[:, d * hw:(d + 1) * hw]
        f = lb + (1.0 - lb) * jax.nn.sigmoid(col(pw + (1 + d) * hw, hw))
        k_ref[...] = (1.0 - f).astype(BF16)
        l_ref[...] = jnp.log(f)
    v_ref[...] = col(pw + 3 * hw, hw).astype(BF16)
    g_ref[...] = _silu(col(pw + 4 * hw, hw)).astype(BF16)


def _proj0_call(x, sc, sh, ng, w, hgl, layer_j, pw, hw):
    bn, ln, d = x.shape
    tm = _row_tile(ln)
    bf = jax.ShapeDtypeStruct((bn, ln, hw), BF16)
    ff = jax.ShapeDtypeStruct((bn, ln, hw), F32)
    return pl.pallas_call(
        functools.partial(_proj0_kernel, layer_j, pw, hw),
        grid=(bn, ln // tm),
        in_specs=[_rows(tm, d), _perb(d), _perb(d), _full((1, d)), _full(w.shape), _full(hgl.shape)],
        out_specs=[_rows(tm, pw)] + [_rows(tm, hw)] * 7,
        out_shape=[jax.ShapeDtypeStruct((bn, ln, pw), BF16), bf, bf, bf, ff, ff, bf, bf],
        compiler_params=_params(("parallel", "parallel")),
        name="proj0",
    )(x, sc, sh, ng, w, hgl)


def _split3(x):
    hi = x.astype(BF16)
    r = x - hi.astype(F32)
    mid = r.astype(BF16)
    lo = (r - mid.astype(F32)).astype(BF16)
    return hi, mid, lo


def _hgrn_kernel(nh, qf_ref, kf_ref, lf_ref, vf_ref, qb_ref, kb_ref, lb_ref, vb_ref, s0f_ref, s0b_ref,
                 of_ref, ob_ref, sff_ref, sfb_ref, stf, stb):
    i = pl.program_id(1)
    c = HG_CHUNK
    nch = qf_ref.shape[0] // c

    @pl.when(i == 0)
    def _():
        stf[...] = s0f_ref[...]
        stb[...] = s0b_ref[...]

    r = lax.broadcasted_iota(jnp.int32, (c, c), 0)
    s = lax.broadcasted_iota(jnp.int32, (c, c), 1)
    lower = s <= r
    upper = s >= r
    tri_f = jnp.where(lower, 1.0, 0.0).astype(BF16)
    tri_b = jnp.where(upper, 1.0, 0.0).astype(BF16)

    def cumul(tri, lf):
        hi, mid, lo = _split3(lf)
        return (jnp.dot(tri, hi, preferred_element_type=F32)
                + jnp.dot(tri, mid, preferred_element_type=F32)
                + jnp.dot(tri, lo, preferred_element_type=F32))

    dirs = ((False, qf_ref, kf_ref, lf_ref, vf_ref, of_ref, stf, tri_f, lower),
            (True, qb_ref, kb_ref, lb_ref, vb_ref, ob_ref, stb, tri_b, upper))

    def rows_of(rev, ci):
        return pl.ds((nch - 1 - ci if rev else ci) * c, c)

    units = [(ci, d, h) for ci in range(nch) for d in range(2) for h in range(nh)]

    decay = {}
    for ci in range(nch):
        for d, (rev, _, _, l_ref, _, _, _, tri, _) in enumerate(dirs):
            decay[ci, d] = cumul(tri, l_ref[rows_of(rev, ci), :])

    q1s, k2s, ebls, scores = {}, {}, {}, {}
    for u in units:
        ci, d, h = u
        rev, q_ref, k_ref = dirs[d][:3]
        rows, cols = rows_of(rev, ci), pl.ds(h * HEAD_DIM, HEAD_DIM)
        b = decay[ci, d][:, h * HEAD_DIM:(h + 1) * HEAD_DIM]
        last = 0 if rev else c - 1
        q1s[u] = (q_ref[rows, cols].astype(F32) * jnp.exp(b)).astype(BF16)
        k1f = k_ref[rows, cols].astype(F32) * jnp.exp(-b)
        ebls[u] = jnp.exp(b[last:last + 1, :])
        k2s[u] = (k1f * ebls[u]).astype(BF16)
        scores[u] = lax.dot_general(q1s[u], k1f.astype(BF16), _NT, preferred_element_type=F32)

    intra, incr = {}, {}
    for u in units:
        ci, d, h = u
        rev, v_ref, mask = dirs[d][0], dirs[d][4], dirs[d][8]
        v = v_ref[rows_of(rev, ci), pl.ds(h * HEAD_DIM, HEAD_DIM)]
        a = jnp.where(mask, scores[u], 0.0).astype(BF16)
        intra[u] = jnp.dot(a, v, preferred_element_type=F32)
        incr[u] = lax.dot_general(v, k2s[u], _TN, preferred_element_type=F32)

    state = {(d, h): dirs[d][6][h] for d in range(2) for h in range(nh)}
    for u in units:
        ci, d, h = u
        rev, o_ref = dirs[d][0], dirs[d][5]
        st = state[d, h]
        o = intra[u] + lax.dot_general(q1s[u], st.astype(BF16), _NT, preferred_element_type=F32)
        o_ref[rows_of(rev, ci), pl.ds(h * HEAD_DIM, HEAD_DIM)] = o.astype(BF16)
        state[d, h] = st * ebls[u] + incr[u]
    for (d, h), st in state.items():
        dirs[d][6][h] = st

    @pl.when(i == pl.num_programs(1) - 1)
    def _():
        sff_ref[...] = stf[...]
        sfb_ref[...] = stb[...]


def _hgrn_call(q, kf, kb, lff, lfb, v, s0f, s0b):
    bn, ln, hw = q.shape
    nh = hw // HEAD_DIM
    tb = 256
    n = ln // tb
    fwd = pl.BlockSpec((None, tb, hw), lambda b, i: (b, i, 0))
    bwd = pl.BlockSpec((None, tb, hw), lambda b, i: (b, n - 1 - i, 0))
    st = pl.BlockSpec((None, nh, HEAD_DIM, HEAD_DIM), lambda b, i: (b, 0, 0, 0))
    st_shape = jax.ShapeDtypeStruct((bn, nh, HEAD_DIM, HEAD_DIM), F32)
    o_shape = jax.ShapeDtypeStruct((bn, ln, hw), BF16)
    return pl.pallas_call(
        functools.partial(_hgrn_kernel, nh),
        grid=(bn, n),
        in_specs=[fwd, fwd, fwd, fwd, bwd, bwd, bwd, bwd, st, st],
        out_specs=[fwd, bwd, st, st],
        out_shape=[o_shape, o_shape, st_shape, st_shape],
        scratch_shapes=[pltpu.VMEM((nh, HEAD_DIM, HEAD_DIM), F32)] * 2,
        compiler_params=_params(("parallel", "arbitrary")),
        name="hgrn_scan",
    )(q, kf, lff, v, q, kb, lfb, v, s0f, s0b)


def _mixer_tail(mix, w_ref, x_ref, g1_ref, ng1_ref, ng2_ref, sc2_ref, sh2_ref, x1_ref, h2_ref):
    y = jnp.dot(mix, w_ref[...], preferred_element_type=F32)
    x1 = x_ref[...] + g1_ref[...] * _rms(y, ng1_ref[...])
    x1_ref[...] = x1
    h2_ref[...] = (_rms(x1, ng2_ref[...]) * (1.0 + sc2_ref[...]) + sh2_ref[...]).astype(BF16)


def _mix0_kernel(seq_len, u_ref, up_ref, un_ref, of_ref, ob_ref, g_ref, pw_ref, ps_ref, og_ref, w_ref,
                 x_ref, g1_ref, ng1_ref, ng2_ref, sc2_ref, sh2_ref, x1_ref, h2_ref, ubuf):
    i = pl.program_id(1)
    tm, pwid = u_ref.shape
    hal = POOL_HALO
    pg = pwid // len(POOL_WINDOWS)

    first = i == 0
    last = i == pl.num_programs(1) - 1
    ubuf[0:hal, :] = jnp.where(first, 0.0, up_ref[...].astype(F32))
    ubuf[hal:hal + tm, :] = u_ref[...].astype(F32)
    ubuf[hal + tm:, :] = jnp.where(last, 0.0, un_ref[...].astype(F32))

    t = i * tm + lax.broadcasted_iota(jnp.int32, (tm, pg), 0)
    parts = []
    for gi, w in enumerate(POOL_WINDOWS):
        cols = pl.ds(gi * pg, pg)
        acc = ubuf[pl.ds(hal - w // 2, tm), cols]
        for jj in range(1, w):
            acc = acc + ubuf[pl.ds(hal - w // 2 + jj, tm), cols]
        lo = jnp.maximum(t - w // 2, 0)
        hi = jnp.minimum(t - w // 2 + w, seq_len)
        y = acc / (hi - lo).astype(F32) - ubuf[pl.ds(hal, tm), cols]
        y = jnp.dot(y.astype(BF16), pw_ref[gi], preferred_element_type=F32)
        parts.append((y * ps_ref[:, gi * pg:(gi + 1) * pg]).astype(BF16))

    osum = of_ref[...].astype(F32) + ob_ref[...].astype(F32)
    gate = g_ref[...].astype(F32)
    for h in range(osum.shape[1] // HEAD_DIM):
        sl = slice(h * HEAD_DIM, (h + 1) * HEAD_DIM)
        parts.append((_rms(osum[:, sl], og_ref[...]) * gate[:, sl]).astype(BF16))

    _mixer_tail(jnp.concatenate(parts, axis=-1), w_ref, x_ref, g1_ref, ng1_ref, ng2_ref, sc2_ref, sh2_ref,
                x1_ref, h2_ref)


def _mix0_call(u, of, ob, g, pool_w, pool_scale, og, w_out, x, g1, ng1, ng2, sc2, sh2):
    bn, ln, d = x.shape
    pwid, hw = u.shape[2], g.shape[2]
    tm = _row_tile(ln)
    hb = tm // POOL_HALO
    nhb = ln // POOL_HALO
    prev = pl.BlockSpec((None, POOL_HALO, pwid), lambda b, i: (b, jnp.maximum(i * hb - 1, 0), 0))
    nxt = pl.BlockSpec((None, POOL_HALO, pwid), lambda b, i: (b, jnp.minimum((i + 1) * hb, nhb - 1), 0))
    return pl.pallas_call(
        functools.partial(_mix0_kernel, ln),
        grid=(bn, ln // tm),
        in_specs=[_rows(tm, pwid), prev, nxt, _rows(tm, hw), _rows(tm, hw), _rows(tm, hw),
                  _full(pool_w.shape), _full((1, pwid)), _full((1, HEAD_DIM)), _full(w_out.shape),
                  _rows(tm, d), _perb(d), _full((1, d)), _full((1, d)), _perb(d), _perb(d)],
        out_specs=[_rows(tm, d), _rows(tm, d)],
        out_shape=[jax.ShapeDtypeStruct((bn, ln, d), F32), jax.ShapeDtypeStruct((bn, ln, d), BF16)],
        scratch_shapes=[pltpu.VMEM((tm + 2 * POOL_HALO, pwid), F32)],
        compiler_params=_params(("parallel", "parallel")),
        name="mix0",
    )(u, u, u, of, ob, g, pool_w, pool_scale, og, w_out, x, g1, ng1, ng2, sc2, sh2)


def _ffn_kernel(hc, x1_ref, h2_ref, wi_ref, wo_ref, g2_ref, ng3_ref, o_ref, t_ref):
    hid = wo_ref.shape[0]
    h2 = h2_ref[...]
    for j in range(hid // hc):
        a = jnp.dot(h2, wi_ref[:, j * hc:(j + 1) * hc], preferred_element_type=F32)
        b = jnp.dot(h2, wi_ref[:, hid + j * hc:hid + (j + 1) * hc], preferred_element_type=F32)
        t_ref[:, j * hc:(j + 1) * hc] = (_silu(a) * b).astype(BF16)
    f = jnp.dot(t_ref[...], wo_ref[...], preferred_element_type=F32)
    o_ref[...] = x1_ref[...] + g2_ref[...] * _rms(f, ng3_ref[...])


def _ffn_call(x1, h2, w_in, w_out, g2, ng3):
    bn, ln, d = x1.shape
    hid = w_out.shape[0]
    tm = _row_tile(ln)
    hc = 256
    assert hid % hc == 0
    return pl.pallas_call(
        functools.partial(_ffn_kernel, hc),
        grid=(bn, ln // tm),
        in_specs=[_rows(tm, d), _rows(tm, d), _full(w_in.shape), _full(w_out.shape), _perb(d), _full((1, d))],
        out_specs=_rows(tm, d),
        out_shape=jax.ShapeDtypeStruct((bn, ln, d), F32),
        scratch_shapes=[pltpu.VMEM((tm, hid), BF16)],
        compiler_params=_params(("parallel", "parallel")),
        name="ffn",
    )(x1, h2, w_in, w_out, g2, ng3)


def _rope_tables(ln):
    half = HEAD_DIM // 2
    n_freq = HEAD_DIM // 4
    pos = np.arange(ln)
    inv = (ROPE_THETA ** (-np.arange(n_freq, dtype=np.float32) / n_freq)).astype(np.float32)
    lane = np.arange(HEAD_DIM)
    p = np.where(lane[None, :] < half, (pos // GRID_W)[:, None], (pos % GRID_W)[:, None]).astype(np.float32)
    ang = (p * inv[lane % n_freq][None, :]).astype(np.float32).astype(np.float64)
    first = (lane % half) < n_freq
    cos = np.cos(ang)
    sin = np.sin(ang)
    sin_up = np.where(first[None, :], -sin, 0.0)
    sin_dn = np.where(first[None, :], 0.0, sin)
    return (jnp.asarray(cos, F32), jnp.asarray(sin_up, F32), jnp.asarray(sin_dn, F32))


def _head_norm_rope(z, g, rope, qscale):
    y = _rms(z, g)
    if rope is not None:
        cos, sin_up, sin_dn = rope
        n_freq = HEAD_DIM // 4
        y = (y * cos + pltpu.roll(y, HEAD_DIM - n_freq, 1) * sin_up + pltpu.roll(y, n_freq, 1) * sin_dn)
    return (y * qscale).astype(BF16)


def _proj1_kernel(with_q, with_rope, nq, nkv, *refs):
    x_ref, sc_ref, sh_ref, ng_ref, w_ref, qg_ref, kg_ref = refs[:7]
    refs = refs[7:]
    rope = None
    if with_rope:
        rope = (refs[0][...], refs[1][...], refs[2][...])
        refs = refs[3:]
    h = (_rms(x_ref[...], ng_ref[...]) * (1.0 + sc_ref[...]) + sh_ref[...]).astype(BF16)
    dh = HEAD_DIM

    def col(idx):
        return jnp.dot(h, w_ref[:, idx * dh:(idx + 1) * dh], preferred_element_type=F32)

    if with_q:
        q_ref, k_ref, v_ref = refs
        for hq in range(nq):
            q_ref[:, hq * dh:(hq + 1) * dh] = _head_norm_rope(col(hq), qg_ref[...], rope, QSCALE)
    else:
        k_ref, v_ref = refs
    for hk in range(nkv):
        k_ref[:, hk * dh:(hk + 1) * dh] = _head_norm_rope(col(nq + hk), kg_ref[...], rope, 1.0)
        v_ref[:, hk * dh:(hk + 1) * dh] = col(nq + nkv + hk).astype(BF16)


def _proj1_call(x, sc, sh, ng, w, qg, kg, with_q, with_rope):
    bn, ln, d = x.shape
    dh = HEAD_DIM
    nkv = ATT_KV_HEADS
    nq = w.shape[1] // dh - 2 * nkv
    tm = _row_tile(ln)
    in_specs = [_rows(tm, d), _perb(d), _perb(d), _full((1, d)), _full(w.shape), _full((1, dh)), _full((1, dh))]
    args = [x, sc, sh, ng, w, qg, kg]
    if with_rope:
        tab = pl.BlockSpec((tm, dh), lambda b, i: (i, 0))
        in_specs += [tab, tab, tab]
        args += list(_rope_tables(ln))
    kv_shape = jax.ShapeDtypeStruct((bn, ln, nkv * dh), BF16)
    out_specs = [_rows(tm, nkv * dh), _rows(tm, nkv * dh)]
    out_shape = [kv_shape, kv_shape]
    if with_q:
        out_specs = [_rows(tm, nq * dh)] + out_specs
        out_shape = [jax.ShapeDtypeStruct((bn, ln, nq * dh), BF16)] + out_shape
    return pl.pallas_call(
        functools.partial(_proj1_kernel, with_q, with_rope, nq, nkv),
        grid=(bn, ln // tm),
        in_specs=in_specs,
        out_specs=out_specs,
        out_shape=out_shape,
        compiler_params=_params(("parallel", "parallel")),
        name="proj1_q" if with_q else "proj1_kv",
    )(*args)


def _attn_kernel(group, tk, q_ref, k_ref, v_ref, o_ref, acc_ref, m_ref):
    dh = HEAD_DIM
    tq = q_ref.shape[0]
    nk = k_ref.shape[0] // tk
    acc_ref[...] = jnp.zeros_like(acc_ref)
    m_ref[...] = jnp.full_like(m_ref, -jnp.inf)
    ones = jnp.ones((tk, dh), BF16)

    def step(j, carry):
        rows = pl.ds(pl.multiple_of(j * tk, tk), tk)
        kc = k_ref[rows, :]
        vc = jnp.concatenate([v_ref[rows, :], ones], axis=-1)
        for g in range(group):
            qg = q_ref[:, g * dh:(g + 1) * dh]
            s = lax.dot_general(qg, kc, _NT, preferred_element_type=F32)
            m_old = m_ref[g]
            m_new = jnp.maximum(m_old, jnp.max(s, axis=1, keepdims=True))
            p = jnp.exp2(s - m_new).astype(BF16)
            alpha = jnp.exp2(m_old - m_new)
            acc_ref[g] = alpha * acc_ref[g] + jnp.dot(p, vc, preferred_element_type=F32)
            m_ref[g] = m_new
        return carry

    lax.fori_loop(0, nk, step, 0)
    for g in range(group):
        acc = acc_ref[g]
        o_ref[:, g * dh:(g + 1) * dh] = (acc[:, :dh] / acc[:, dh:dh + 1]).astype(BF16)


def _attn_call(q, k, v, tq, tk):
    bn, lq, dq = q.shape
    lk = k.shape[1]
    dh = HEAD_DIM
    nkv = k.shape[2] // dh
    group = dq // dh // nkv
    assert lq % tq == 0 and lk % tk == 0
    qspec = pl.BlockSpec((None, tq, group * dh), lambda b, h, i: (b, i, h))
    kvspec = pl.BlockSpec((None, lk, dh), lambda b, h, i: (b, 0, h))
    return pl.pallas_call(
        functools.partial(_attn_kernel, group, tk),
        grid=(bn, nkv, lq // tq),
        in_specs=[qspec, kvspec, kvspec],
        out_specs=qspec,
        out_shape=jax.ShapeDtypeStruct((bn, lq, dq), BF16),
        scratch_shapes=[pltpu.VMEM((group, tq, 2 * dh), F32), pltpu.VMEM((group, tq, 1), F32)],
        compiler_params=_params(("parallel", "parallel", "parallel")),
        name="gqa_attn",
    )(q, k, v)


def _attn_fast_kernel(group, bound_ref, q_ref, k_ref, vt_ref, o_ref, qt_ref, acc_ref):
    dh = HEAD_DIM
    nk, _, tk = vt_ref.shape
    for g in range(group):
        qt_ref[g] = q_ref[:, g * dh:(g + 1) * dh].astype(F32).T.astype(BF16)
    acc_ref[...] = jnp.zeros_like(acc_ref)
    ones = jnp.ones((ONES_ROWS, tk), BF16)
    bound = bound_ref[0]

    def step(j, carry):
        vext = jnp.concatenate([vt_ref[j], ones], axis=0)
        base = pl.multiple_of(j * tk, tk)
        work = [(s, g) for s in range(tk // ATT_SUB) for g in range(group)]

        def scores(s, g):
            kc = k_ref[pl.ds(base + s * ATT_SUB, ATT_SUB), :]
            return jnp.dot(kc, qt_ref[g], preferred_element_type=F32)

        pending = [scores(*w) for w in work[:ATT_AHEAD]]
        tot = [None] * group
        for idx, (s, g) in enumerate(work):
            if idx + ATT_AHEAD < len(work):
                pending.append(scores(*work[idx + ATT_AHEAD]))
            pt = jnp.exp2(pending.pop(0) - bound).astype(BF16)
            pv = jnp.dot(vext[:, s * ATT_SUB:(s + 1) * ATT_SUB], pt, preferred_element_type=F32)
            tot[g] = pv if tot[g] is None else tot[g] + pv
        for g in range(group):
            acc_ref[g] += tot[g]
        return carry

    lax.fori_loop(0, nk, step, 0)
    for g in range(group):
        acc = acc_ref[g]
        o_ref[:, g * dh:(g + 1) * dh] = (acc[:dh] / acc[dh:dh + 1]).T.astype(BF16)


def _attn_fast_call(bound, q, k, v, tq, tk):
    bn, lq, dq = q.shape
    lk = k.shape[1]
    dh = HEAD_DIM
    nkv = k.shape[2] // dh
    group = dq // dh // nkv
    assert lq % tq == 0 and lk % tk == 0
    nk = lk // tk
    vt = v.reshape(bn, nk, tk, nkv, dh).transpose(0, 3, 1, 4, 2)
    qspec = pl.BlockSpec((None, tq, group * dh), lambda b, h, i: (b, i, h))
    return pl.pallas_call(
        functools.partial(_attn_fast_kernel, group),
        grid=(bn, nkv, lq // tq),
        in_specs=[pl.BlockSpec(memory_space=pltpu.SMEM), qspec,
                  pl.BlockSpec((None, lk, dh), lambda b, h, i: (b, 0, h)),
                  pl.BlockSpec((None, None, nk, dh, tk), lambda b, h, i: (b, h, 0, 0, 0))],
        out_specs=qspec,
        out_shape=jax.ShapeDtypeStruct((bn, lq, dq), BF16),
        scratch_shapes=[pltpu.VMEM((group, dh, tq), BF16), pltpu.VMEM((group, dh + ONES_ROWS, tq), F32)],
        compiler_params=_params(("parallel", "parallel", "parallel")),
        name="gqa_attn_fast",
    )(bound, q, k, vt)


def _score_bound(qg, kg):
    return (1.02 * HEAD_DIM * QSCALE) * jnp.max(jnp.abs(qg)) * jnp.max(jnp.abs(kg))


def _attention(q, k, v, qg, kg):
    lk = k.shape[1]
    tk_fast = max(t for t in range(ATT_SUB, min(lk, ATT_MAX_TRIP) + 1, ATT_SUB) if lk % t == 0)
    bound = _score_bound(qg, kg).reshape(1)
    return lax.cond(bound[0] <= SAFE_BOUND,
                    lambda: _attn_fast_call(bound, q, k, v, 256, tk_fast),
                    lambda: _attn_call(q, k, v, 256, _kv_chunk(lk)))


def _kv_chunk(lk):
    for tk in (768, 512, 384, 256, 128):
        if lk % tk == 0:
            return tk
    raise ValueError(f"key length {lk} is not a multiple of 128")


def _mix1_kernel(o_ref, w_ref, x_ref, g1_ref, ng1_ref, ng2_ref, sc2_ref, sh2_ref, x1_ref, h2_ref):
    _mixer_tail(o_ref[...], w_ref, x_ref, g1_ref, ng1_ref, ng2_ref, sc2_ref, sh2_ref, x1_ref, h2_ref)


def _mix1_call(o, w_out, x, g1, ng1, ng2, sc2, sh2):
    bn, ln, d = x.shape
    tm = _row_tile(ln)
    return pl.pallas_call(
        _mix1_kernel,
        grid=(bn, ln // tm),
        in_specs=[_rows(tm, o.shape[2]), _full(w_out.shape), _rows(tm, d), _perb(d), _full((1, d)),
                  _full((1, d)), _perb(d), _perb(d)],
        out_specs=[_rows(tm, d), _rows(tm, d)],
        out_shape=[jax.ShapeDtypeStruct((bn, ln, d), F32), jax.ShapeDtypeStruct((bn, ln, d), BF16)],
        compiler_params=_params(("parallel", "parallel")),
        name="mix1",
    )(o, w_out, x, g1, ng1, ng2, sc2, sh2)


def kernel(x, c, ctx, c_ctx, ada_w, ada_b, norm_g, ab_w_in, ab_w_out, pool_w, pool_scale, hg_lower, hg_onorm_g,
           att_w_in, att_w_out, att_qnorm_g, att_knorm_g, ffn_w_in, ffn_w_out):
    bn, ln, d = x.shape
    depth = ada_w.shape[0]
    hw = hg_lower.shape[2]
    pwid = pool_scale.shape[1]
    nh = hw // HEAD_DIM

    pad = (-(bn + 1)) % 8
    cc = jnp.concatenate([c, c_ctx[None, :], jnp.zeros((pad, d), F32)], axis=0)
    mods = _ada_call(cc, ada_w, ada_b)

    def split_mods(l, ctx_rows):
        m = mods[l, bn:bn + 1] if ctx_rows else mods[l, :bn]
        m = jnp.broadcast_to(m, (bn, 6 * d)).reshape(bn, 1, 6, d)
        return [m[:, :, t, :] for t in range(6)]

    lat, cx = x, ctx
    for l in range(depth):
        j = l // 2
        need_ctx = l < depth - 1
        ng = [norm_g[l, t][None, :] for t in range(4)]
        m_lat = split_mods(l, False)
        m_ctx = split_mods(l, True)
        w_ffn_in = ffn_w_in[l].astype(BF16)
        w_ffn_out = ffn_w_out[l].astype(BF16)

        if l % 2 == 0:
            w_in = ab_w_in[j].astype(BF16)
            w_out = ab_w_out[j].astype(BF16)
            pw = pool_w[j].astype(BF16)
            ps = pool_scale[j][None, :]
            og = hg_onorm_g[j][None, :]
            hgl = hg_lower.reshape(hg_lower.shape[0], 2 * hw)
            pc = _proj0_call(cx, m_ctx[1], m_ctx[0], ng[0], w_in, hgl, j, pwid, hw)
            plat = _proj0_call(lat, m_lat[1], m_lat[0], ng[0], w_in, hgl, j, pwid, hw)
            zero = jnp.zeros((bn, nh, HEAD_DIM, HEAD_DIM), F32)
            u_c, q_c, kf_c, kb_c, lff_c, lfb_c, v_c, g_c = pc
            u_l, q_l, kf_l, kb_l, lff_l, lfb_l, v_l, g_l = plat
            of_c, ob_c, s_f, s_b = _hgrn_call(q_c, kf_c, kb_c, lff_c, lfb_c, v_c, zero, zero)
            of_l, ob_l, _, _ = _hgrn_call(q_l, kf_l, kb_l, lff_l, lfb_l, v_l, s_f, s_b)
            x1_l, h2_l = _mix0_call(u_l, of_l, ob_l, g_l, pw, ps, og, w_out, lat, m_lat[2], ng[1], ng[2],
                                    m_lat[4], m_lat[3])
            if need_ctx:
                x1_c, h2_c = _mix0_call(u_c, of_c, ob_c, g_c, pw, ps, og, w_out, cx, m_ctx[2], ng[1], ng[2],
                                        m_ctx[4], m_ctx[3])
        else:
            w_in = att_w_in[j].astype(BF16)
            w_out = att_w_out[j].astype(BF16)
            qg = att_qnorm_g[j][None, :]
            kg = att_knorm_g[j][None, :]
            outs_c = _proj1_call(cx, m_ctx[1], m_ctx[0], ng[0], w_in, qg, kg, need_ctx, False)
            q_l, k_l, v_l = _proj1_call(lat, m_lat[1], m_lat[0], ng[0], w_in, qg, kg, True, True)
            k_c, v_c = outs_c[-2], outs_c[-1]
            k_all = jnp.concatenate([k_l, k_c], axis=1)
            v_all = jnp.concatenate([v_l, v_c], axis=1)
            o_l = _attention(q_l, k_all, v_all, qg, kg)
            x1_l, h2_l = _mix1_call(o_l, w_out, lat, m_lat[2], ng[1], ng[2], m_lat[4], m_lat[3])
            if need_ctx:
                o_c = _attn_call(outs_c[0], k_c, v_c, 256, _kv_chunk(k_c.shape[1]))
                x1_c, h2_c = _mix1_call(o_c, w_out, cx, m_ctx[2], ng[1], ng[2], m_ctx[4], m_ctx[3])

        lat = _ffn_call(x1_l, h2_l, w_ffn_in, w_ffn_out, m_lat[5], ng[3])
        if need_ctx:
            cx = _ffn_call(x1_c, h2_c, w_ffn_in, w_ffn_out, m_ctx[5], ng[3])
    return lat
```

```python
import functools

import numpy as np
import jax
import jax.numpy as jnp
from jax import lax
from jax.experimental import pallas as pl
from jax.experimental.pallas import tpu as pltpu

F32 = jnp.float32
BF16 = jnp.bfloat16

RMS_EPS = 1e-6
GRID_W = 64
POOL_WINDOWS = (2, 4, 8, 16)
HEAD_DIM = 128
ATT_KV_HEADS = 2
ROPE_THETA = 10000.0
HG_CHUNK = 64
HG_SAFE_CHUNK = 16
HG_SAFE_DECAY = 80.0
TAIL_ROWS = 256
POOL_HALO = 16
LOG2E = 1.4426950408889634
QSCALE = HEAD_DIM ** -0.5 * LOG2E
ONES_ROWS = 16
ATT_SUB = 256
ATT_AHEAD = 3
ATT_MAX_TRIP = 8448
SAFE_BOUND = 60.0

VMEM_LIMIT = 56 * 1024 * 1024

_NT = (((1,), (1,)), ((), ()))
_TN = (((0,), (0,)), ((), ()))


def _params(sem):
    return pltpu.CompilerParams(dimension_semantics=sem, vmem_limit_bytes=VMEM_LIMIT)


def _rows(tm, w):
    return pl.BlockSpec((None, tm, w), lambda b, i: (b, i, 0))


def _perb(w):
    return pl.BlockSpec((None, 1, w), lambda b, i: (b, 0, 0))


def _full(shape):
    nd = len(shape)
    return pl.BlockSpec(shape, lambda b, i: (0,) * nd, pipeline_mode=pl.Buffered(1))


def _sigmoid(z):
    return 0.5 * jnp.tanh(0.5 * z) + 0.5


def _silu(z):
    return z * _sigmoid(z)


def _rms(x, g):
    return x * lax.rsqrt(jnp.mean(x * x, axis=-1, keepdims=True) + RMS_EPS) * g


def _row_tile(n):
    return 512 if n % 512 == 0 else 256


def _ada_kernel(c_ref, w_ref, b_ref, o_ref):
    o_ref[...] = jnp.dot(_silu(c_ref[...]), w_ref[...], preferred_element_type=F32) + b_ref[...]


def _ada_call(cc, ada_w, ada_b):
    depth, d, n6 = ada_w.shape
    rows = cc.shape[0]
    tn = n6 // 4
    return pl.pallas_call(
        _ada_kernel,
        grid=(depth, n6 // tn),
        in_specs=[
            pl.BlockSpec((rows, d), lambda l, j: (0, 0)),
            pl.BlockSpec((None, d, tn), lambda l, j: (l, 0, j)),
            pl.BlockSpec((None, 1, tn), lambda l, j: (l, 0, j)),
        ],
        out_specs=pl.BlockSpec((None, rows, tn), lambda l, j: (l, 0, j)),
        out_shape=jax.ShapeDtypeStruct((depth, rows, n6), F32),
        compiler_params=_params(("arbitrary", "arbitrary")),
        name="ada_mod",
    )(cc, ada_w, ada_b.reshape(depth, 1, n6))


def _proj0_kernel(layer_j, pw, hw, x_ref, sc_ref, sh_ref, ng_ref, w_ref, hgl_ref,
                  u_ref, q_ref, kf_ref, kb_ref, lff_ref, lfb_ref, v_ref, g_ref):
    h = (_rms(x_ref[...], ng_ref[...]) * (1.0 + sc_ref[...]) + sh_ref[...]).astype(BF16)

    def col(lo, width):
        return jnp.dot(h, w_ref[:, lo:lo + width], preferred_element_type=F32)

    hgl = hgl_ref[...]
    e = jnp.exp(hgl - jnp.max(hgl, axis=0, keepdims=True))
    sm = e / jnp.sum(e, axis=0, keepdims=True)
    lbound = jnp.sum(sm[:layer_j + 1], axis=0, keepdims=True)

    def gate(d, k_ref, l_ref):
        def finish(z):
            lb = lbound[:, d * hw:(d + 1) * hw]
            f = lb + (1.0 - lb) * _sigmoid(z)
            k_ref[...] = (1.0 - f).astype(BF16)
            l_ref[...] = jnp.log(f)
        return finish

    def plain(o_ref, act):
        def finish(z):
            o_ref[...] = act(z).astype(BF16)
        return finish

    blocks = [(0, pw, plain(u_ref, lambda z: z)), (pw, hw, plain(q_ref, _silu)),
              (pw + hw, hw, gate(0, kf_ref, lff_ref)), (pw + 2 * hw, hw, gate(1, kb_ref, lfb_ref)),
              (pw + 3 * hw, hw, plain(v_ref, lambda z: z)), (pw + 4 * hw, hw, plain(g_ref, _silu))]
    z = col(*blocks[0][:2])
    for idx, (_, _, finish) in enumerate(blocks):
        z_next = col(*blocks[idx + 1][:2]) if idx + 1 < len(blocks) else None
        finish(z)
        z = z_next


def _proj0_call(x, sc, sh, ng, w, hgl, layer_j, pw, hw):
    bn, ln, d = x.shape
    tm = _row_tile(ln)
    bf = jax.ShapeDtypeStruct((bn, ln, hw), BF16)
    ff = jax.ShapeDtypeStruct((bn, ln, hw), F32)
    return pl.pallas_call(
        functools.partial(_proj0_kernel, layer_j, pw, hw),
        grid=(bn, ln // tm),
        in_specs=[_rows(tm, d), _perb(d), _perb(d), _full((1, d)), _full(w.shape), _full(hgl.shape)],
        out_specs=[_rows(tm, pw)] + [_rows(tm, hw)] * 7,
        out_shape=[jax.ShapeDtypeStruct((bn, ln, pw), BF16), bf, bf, bf, ff, ff, bf, bf],
        compiler_params=_params(("parallel", "parallel")),
        name="proj0",
    )(x, sc, sh, ng, w, hgl)


def _split3(x):
    hi = x.astype(BF16)
    r = x - hi.astype(F32)
    mid = r.astype(BF16)
    lo = (r - mid.astype(F32)).astype(BF16)
    return hi, mid, lo


def _direct_scores(q, k, b):
    c = q.shape[0]
    lane = lax.broadcasted_iota(jnp.int32, (c, c), 1)
    a = jnp.zeros((c, c), F32)
    for s in range(c):
        e = jnp.exp(jnp.minimum(b - b[s:s + 1, :], 0.0))
        col = jnp.sum(q * e * k[s:s + 1, :], axis=1, keepdims=True)
        a = jnp.where(lane == s, col, a)
    return a


def _hgrn_kernel(nh, c, direct, qf_ref, kf_ref, lf_ref, vf_ref, qb_ref, kb_ref, lb_ref, vb_ref, s0f_ref, s0b_ref,
                 of_ref, ob_ref, sff_ref, sfb_ref, stf, stb):
    i = pl.program_id(1)
    nch = qf_ref.shape[0] // c

    @pl.when(i == 0)
    def _():
        stf[...] = s0f_ref[...]
        stb[...] = s0b_ref[...]

    r = lax.broadcasted_iota(jnp.int32, (c, c), 0)
    s = lax.broadcasted_iota(jnp.int32, (c, c), 1)
    lower = s <= r
    upper = s >= r
    tri_f = jnp.where(lower, 1.0, 0.0).astype(BF16)
    tri_b = jnp.where(upper, 1.0, 0.0).astype(BF16)

    def cumul(tri, lf):
        hi, mid, lo = _split3(lf)
        return (jnp.dot(tri, hi, preferred_element_type=F32)
                + jnp.dot(tri, mid, preferred_element_type=F32)
                + jnp.dot(tri, lo, preferred_element_type=F32))

    dirs = ((False, qf_ref, kf_ref, lf_ref, vf_ref, of_ref, stf, tri_f, lower),
            (True, qb_ref, kb_ref, lb_ref, vb_ref, ob_ref, stb, tri_b, upper))

    def rows_of(rev, ci):
        return pl.ds((nch - 1 - ci if rev else ci) * c, c)

    units = [(ci, d, h) for ci in range(nch) for d in range(2) for h in range(nh)]

    decay = {}
    for ci in range(nch):
        for d, (rev, _, _, l_ref, _, _, _, tri, _) in enumerate(dirs):
            decay[ci, d] = cumul(tri, l_ref[rows_of(rev, ci), :])

    q1s, k2s, ebls, scores = {}, {}, {}, {}
    for u in units:
        ci, d, h = u
        rev, q_ref, k_ref = dirs[d][:3]
        rows, cols = rows_of(rev, ci), pl.ds(h * HEAD_DIM, HEAD_DIM)
        b = decay[ci, d][:, h * HEAD_DIM:(h + 1) * HEAD_DIM]
        last = 0 if rev else c - 1
        qf = q_ref[rows, cols].astype(F32)
        kf = k_ref[rows, cols].astype(F32)
        q1s[u] = (qf * jnp.exp(b)).astype(BF16)
        ebls[u] = jnp.exp(b[last:last + 1, :])
        if direct:
            k2s[u] = (kf * jnp.exp(b[last:last + 1, :] - b)).astype(BF16)
            scores[u] = _direct_scores(qf, kf, b)
        else:
            k1f = kf * jnp.exp(-b)
            k2s[u] = (k1f * ebls[u]).astype(BF16)
            scores[u] = lax.dot_general(q1s[u], k1f.astype(BF16), _NT, preferred_element_type=F32)

    intra, incr = {}, {}
    for u in units:
        ci, d, h = u
        rev, v_ref, mask = dirs[d][0], dirs[d][4], dirs[d][8]
        v = v_ref[rows_of(rev, ci), pl.ds(h * HEAD_DIM, HEAD_DIM)]
        a = jnp.where(mask, scores[u], 0.0).astype(BF16)
        intra[u] = jnp.dot(a, v, preferred_element_type=F32)
        incr[u] = lax.dot_general(v, k2s[u], _TN, preferred_element_type=F32)

    state = {(d, h): dirs[d][6][h] for d in range(2) for h in range(nh)}
    for u in units:
        ci, d, h = u
        rev, o_ref = dirs[d][0], dirs[d][5]
        st = state[d, h]
        o = intra[u] + lax.dot_general(q1s[u], st.astype(BF16), _NT, preferred_element_type=F32)
        o_ref[rows_of(rev, ci), pl.ds(h * HEAD_DIM, HEAD_DIM)] = o.astype(BF16)
        state[d, h] = st * ebls[u] + incr[u]
    for (d, h), st in state.items():
        dirs[d][6][h] = st

    @pl.when(i == pl.num_programs(1) - 1)
    def _():
        sff_ref[...] = stf[...]
        sfb_ref[...] = stb[...]


def _hgrn_call(q, kf, kb, lff, lfb, v, s0f, s0b, chunk, direct, tb):
    bn, ln, hw = q.shape
    nh = hw // HEAD_DIM
    assert ln % tb == 0 and tb % chunk == 0
    n = ln // tb
    fwd = pl.BlockSpec((None, tb, hw), lambda b, i: (b, i, 0))
    bwd = pl.BlockSpec((None, tb, hw), lambda b, i: (b, n - 1 - i, 0))
    st = pl.BlockSpec((None, nh, HEAD_DIM, HEAD_DIM), lambda b, i: (b, 0, 0, 0))
    st_shape = jax.ShapeDtypeStruct((bn, nh, HEAD_DIM, HEAD_DIM), F32)
    o_shape = jax.ShapeDtypeStruct((bn, ln, hw), BF16)
    return pl.pallas_call(
        functools.partial(_hgrn_kernel, nh, chunk, direct),
        grid=(bn, n),
        in_specs=[fwd, fwd, fwd, fwd, bwd, bwd, bwd, bwd, st, st],
        out_specs=[fwd, bwd, st, st],
        out_shape=[o_shape, o_shape, st_shape, st_shape],
        scratch_shapes=[pltpu.VMEM((nh, HEAD_DIM, HEAD_DIM), F32)] * 2,
        compiler_params=_params(("parallel", "arbitrary")),
        name="hgrn_scan",
    )(q, kf, lff, v, q, kb, lfb, v, s0f, s0b)


def _mixer_tail(mix, w_ref, x_ref, g1_ref, ng1_ref, ng2_ref, sc2_ref, sh2_ref, x1_ref, h2_ref):
    tm = mix.shape[0]
    rb = min(tm, TAIL_ROWS)
    gain1 = g1_ref[...] * ng1_ref[...]
    gain2 = ng2_ref[...] * (1.0 + sc2_ref[...])
    for r in range(tm // rb):
        rows = pl.ds(r * rb, rb)
        y = jnp.dot(mix[r * rb:(r + 1) * rb], w_ref[...], preferred_element_type=F32)
        x1 = x_ref[rows, :] + _rms(y, gain1)
        x1_ref[rows, :] = x1
        h2_ref[rows, :] = (_rms(x1, gain2) + sh2_ref[...]).astype(BF16)


def _mix0_kernel(seq_len, u_ref, up_ref, un_ref, of_ref, ob_ref, g_ref, pw_ref, ps_ref, og_ref, w_ref,
                 x_ref, g1_ref, ng1_ref, ng2_ref, sc2_ref, sh2_ref, x1_ref, h2_ref, ubuf):
    i = pl.program_id(1)
    tm, pwid = u_ref.shape
    hal = POOL_HALO
    pg = pwid // len(POOL_WINDOWS)

    first = i == 0
    last = i == pl.num_programs(1) - 1
    ubuf[0:hal, :] = jnp.where(first, 0.0, up_ref[...].astype(F32))
    ubuf[hal:hal + tm, :] = u_ref[...].astype(F32)
    ubuf[hal + tm:, :] = jnp.where(last, 0.0, un_ref[...].astype(F32))

    edge = max(POOL_WINDOWS) // 2
    t_top = i * tm + lax.broadcasted_iota(jnp.int32, (edge, pg), 0)
    t_bot = t_top + (tm - edge)

    def clipped_mean(acc_rows, t, w):
        lo = jnp.maximum(t - w // 2, 0)
        hi = jnp.minimum(t - w // 2 + w, seq_len)
        return acc_rows / (hi - lo).astype(F32)

    parts = []
    for gi, w in enumerate(POOL_WINDOWS):
        cols = pl.ds(gi * pg, pg)
        acc = ubuf[pl.ds(hal - w // 2, tm), cols]
        for jj in range(1, w):
            acc = acc + ubuf[pl.ds(hal - w // 2 + jj, tm), cols]
        mean = jnp.concatenate([clipped_mean(acc[:edge], t_top, w), acc[edge:tm - edge] * (1.0 / w),
                                clipped_mean(acc[tm - edge:], t_bot, w)], axis=0)
        y = mean - ubuf[pl.ds(hal, tm), cols]
        y = jnp.dot(y.astype(BF16), pw_ref[gi], preferred_element_type=F32)
        parts.append((y * ps_ref[:, gi * pg:(gi + 1) * pg]).astype(BF16))

    osum = of_ref[...].astype(F32) + ob_ref[...].astype(F32)
    gate = g_ref[...].astype(F32)
    for h in range(osum.shape[1] // HEAD_DIM):
        sl = slice(h * HEAD_DIM, (h + 1) * HEAD_DIM)
        parts.append((_rms(osum[:, sl], og_ref[...]) * gate[:, sl]).astype(BF16))

    _mixer_tail(jnp.concatenate(parts, axis=-1), w_ref, x_ref, g1_ref, ng1_ref, ng2_ref, sc2_ref, sh2_ref,
                x1_ref, h2_ref)


def _mix0_call(u, of, ob, g, pool_w, pool_scale, og, w_out, x, g1, ng1, ng2, sc2, sh2):
    bn, ln, d = x.shape
    pwid, hw = u.shape[2], g.shape[2]
    tm = _row_tile(ln)
    hb = tm // POOL_HALO
    nhb = ln // POOL_HALO
    prev = pl.BlockSpec((None, POOL_HALO, pwid), lambda b, i: (b, jnp.maximum(i * hb - 1, 0), 0))
    nxt = pl.BlockSpec((None, POOL_HALO, pwid), lambda b, i: (b, jnp.minimum((i + 1) * hb, nhb - 1), 0))
    return pl.pallas_call(
        functools.partial(_mix0_kernel, ln),
        grid=(bn, ln // tm),
        in_specs=[_rows(tm, pwid), prev, nxt, _rows(tm, hw), _rows(tm, hw), _rows(tm, hw),
                  _full(pool_w.shape), _full((1, pwid)), _full((1, HEAD_DIM)), _full(w_out.shape),
                  _rows(tm, d), _perb(d), _full((1, d)), _full((1, d)), _perb(d), _perb(d)],
        out_specs=[_rows(tm, d), _rows(tm, d)],
        out_shape=[jax.ShapeDtypeStruct((bn, ln, d), F32), jax.ShapeDtypeStruct((bn, ln, d), BF16)],
        scratch_shapes=[pltpu.VMEM((tm + 2 * POOL_HALO, pwid), F32)],
        compiler_params=_params(("parallel", "parallel")),
        name="mix0",
    )(u, u, u, of, ob, g, pool_w, pool_scale, og, w_out, x, g1, ng1, ng2, sc2, sh2)


def _ffn_kernel(hc, x1_ref, h2_ref, wi_ref, wo_ref, g2_ref, ng3_ref, o_ref, t_ref):
    hid = wo_ref.shape[0]
    h2 = h2_ref[...]
    for j in range(hid // hc):
        a = jnp.dot(h2, wi_ref[:, j * hc:(j + 1) * hc], preferred_element_type=F32)
        b = jnp.dot(h2, wi_ref[:, hid + j * hc:hid + (j + 1) * hc], preferred_element_type=F32)
        t_ref[:, j * hc:(j + 1) * hc] = (_silu(a) * b).astype(BF16)
    f = jnp.dot(t_ref[...], wo_ref[...], preferred_element_type=F32)
    o_ref[...] = x1_ref[...] + g2_ref[...] * _rms(f, ng3_ref[...])


def _ffn_call(x1, h2, w_in, w_out, g2, ng3):
    bn, ln, d = x1.shape
    hid = w_out.shape[0]
    tm = _row_tile(ln)
    hc = 256
    assert hid % hc == 0
    return pl.pallas_call(
        functools.partial(_ffn_kernel, hc),
        grid=(bn, ln // tm),
        in_specs=[_rows(tm, d), _rows(tm, d), _full(w_in.shape), _full(w_out.shape), _perb(d), _full((1, d))],
        out_specs=_rows(tm, d),
        out_shape=jax.ShapeDtypeStruct((bn, ln, d), F32),
        scratch_shapes=[pltpu.VMEM((tm, hid), BF16)],
        compiler_params=_params(("parallel", "parallel")),
        name="ffn",
    )(x1, h2, w_in, w_out, g2, ng3)


def _head_perm():
    n_freq = HEAD_DIM // 4
    blocks = np.arange(HEAD_DIM).reshape(2, 2, n_freq)
    return blocks.transpose(1, 0, 2).reshape(-1)


def _rope_tables(ln):
    half = HEAD_DIM // 2
    n_freq = HEAD_DIM // 4
    pos = np.arange(ln)
    inv = (ROPE_THETA ** (-np.arange(n_freq, dtype=np.float32) / n_freq)).astype(np.float32)
    lane = _head_perm()
    p = np.where(lane[None, :] < half, (pos // GRID_W)[:, None], (pos % GRID_W)[:, None]).astype(np.float32)
    ang = (p * inv[lane % n_freq][None, :]).astype(np.float32).astype(np.float64)
    sign = np.where(np.arange(HEAD_DIM) < half, -1.0, 1.0)
    return jnp.asarray(np.cos(ang), F32), jnp.asarray(np.sin(ang) * sign[None, :], F32)


def _gain_tables(g, rope):
    if rope is None:
        return jnp.concatenate([g, g], axis=1), None
    cos, sin = rope
    g_rolled = pltpu.roll(jnp.broadcast_to(g, (8, HEAD_DIM)), HEAD_DIM // 2, 1)[:1]
    gcos, gsin = cos * g, sin * g_rolled
    return jnp.concatenate([gcos, gcos], axis=1), jnp.concatenate([gsin, gsin], axis=1)


def _pair_matrices():
    n = 2 * HEAD_DIM
    r = lax.broadcasted_iota(jnp.int32, (n, n), 0)
    c = lax.broadcasted_iota(jnp.int32, (n, n), 1)
    same_head = (r // HEAD_DIM) == (c // HEAD_DIM)
    ones = jnp.where(same_head, 1.0, 0.0).astype(BF16)
    swap = jnp.where(same_head & ((r % HEAD_DIM) == ((c + HEAD_DIM // 2) % HEAD_DIM)), 1.0, 0.0).astype(BF16)
    return ones, swap


def _pair_stats(z, with_swap, ones, swap):
    ssq = jnp.dot((z * z).astype(BF16), ones, preferred_element_type=F32)
    swapped = jnp.dot(z.astype(BF16), swap, preferred_element_type=F32) if with_swap else None
    return ssq, swapped


def _pair_finish(z, stats, tables):
    gcos, gsin = tables
    ssq, swapped = stats
    y = z * gcos
    if swapped is not None:
        y = y + swapped * gsin
    return (y * lax.rsqrt(ssq * (1.0 / HEAD_DIM) + RMS_EPS)).astype(BF16)


def _proj1_kernel(with_q, with_rope, nq, nkv, *refs):
    x_ref, sc_ref, sh_ref, ng_ref, w_ref, qg_ref, kg_ref = refs[:7]
    refs = refs[7:]
    rope = None
    if with_rope:
        rope = (refs[0][...], refs[1][...])
        refs = refs[2:]
    h = (_rms(x_ref[...], ng_ref[...]) * (1.0 + sc_ref[...]) + sh_ref[...]).astype(BF16)
    dh = HEAD_DIM

    first = 0 if with_q else nq
    z = jnp.dot(h, w_ref[:, first * dh:], preferred_element_type=F32)

    def pair(hd):
        return z[:, (hd - first) * dh:(hd - first + 2) * dh]

    ones, swap = _pair_matrices()
    if with_q:
        q_ref, k_ref, v_ref = refs
        q_tables = _gain_tables(qg_ref[...] * QSCALE, rope)
    else:
        k_ref, v_ref = refs
    k_tables = _gain_tables(kg_ref[...], rope)
    todo = [(hq, q_ref, hq, q_tables) for hq in range(0, nq if with_q else 0, 2)]
    todo += [(nq + hk, k_ref, hk, k_tables) for hk in range(0, nkv, 2)]
    stats = [_pair_stats(pair(hd), with_rope, ones, swap) for hd, _, _, _ in todo]
    for (hd, o_ref, ho, tables), st in zip(todo, stats):
        o_ref[:, ho * dh:(ho + 2) * dh] = _pair_finish(pair(hd), st, tables)
    for hk in range(0, nkv, 2):
        v_ref[:, hk * dh:(hk + 2) * dh] = pair(nq + nkv + hk).astype(BF16)


def _proj1_call(x, sc, sh, ng, w, qg, kg, with_q, with_rope):
    bn, ln, d = x.shape
    dh = HEAD_DIM
    nkv = ATT_KV_HEADS
    nq = w.shape[1] // dh - 2 * nkv
    tm = _row_tile(ln)
    in_specs = [_rows(tm, d), _perb(d), _perb(d), _full((1, d)), _full(w.shape), _full((1, dh)), _full((1, dh))]
    args = [x, sc, sh, ng, w, qg, kg]
    if with_rope:
        tab = pl.BlockSpec((tm, dh), lambda b, i: (i, 0))
        in_specs += [tab, tab]
        args += list(_rope_tables(ln))
    kv_shape = jax.ShapeDtypeStruct((bn, ln, nkv * dh), BF16)
    out_specs = [_rows(tm, nkv * dh), _rows(tm, nkv * dh)]
    out_shape = [kv_shape, kv_shape]
    if with_q:
        out_specs = [_rows(tm, nq * dh)] + out_specs
        out_shape = [jax.ShapeDtypeStruct((bn, ln, nq * dh), BF16)] + out_shape
    return pl.pallas_call(
        functools.partial(_proj1_kernel, with_q, with_rope, nq, nkv),
        grid=(bn, ln // tm),
        in_specs=in_specs,
        out_specs=out_specs,
        out_shape=out_shape,
        compiler_params=_params(("parallel", "parallel")),
        name="proj1_q" if with_q else "proj1_kv",
    )(*args)


def _attn_kernel(group, tk, q_ref, k_ref, v_ref, o_ref, acc_ref, m_ref):
    dh = HEAD_DIM
    tq = q_ref.shape[0]
    nk = k_ref.shape[0] // tk
    acc_ref[...] = jnp.zeros_like(acc_ref)
    m_ref[...] = jnp.full_like(m_ref, -jnp.inf)
    ones = jnp.ones((tk, dh), BF16)

    def step(j, carry):
        rows = pl.ds(pl.multiple_of(j * tk, tk), tk)
        kc = k_ref[rows, :]
        vc = jnp.concatenate([v_ref[rows, :], ones], axis=-1)
        for g in range(group):
            qg = q_ref[:, g * dh:(g + 1) * dh]
            s = lax.dot_general(qg, kc, _NT, preferred_element_type=F32)
            m_old = m_ref[g]
            m_new = jnp.maximum(m_old, jnp.max(s, axis=1, keepdims=True))
            p = jnp.exp2(s - m_new).astype(BF16)
            alpha = jnp.exp2(m_old - m_new)
            acc_ref[g] = alpha * acc_ref[g] + jnp.dot(p, vc, preferred_element_type=F32)
            m_ref[g] = m_new
        return carry

    lax.fori_loop(0, nk, step, 0)
    for g in range(group):
        acc = acc_ref[g]
        o_ref[:, g * dh:(g + 1) * dh] = (acc[:, :dh] / acc[:, dh:dh + 1]).astype(BF16)


def _attn_call(q, k, v, tq, tk):
    bn, lq, dq = q.shape
    lk = k.shape[1]
    dh = HEAD_DIM
    nkv = k.shape[2] // dh
    group = dq // dh // nkv
    assert lq % tq == 0 and lk % tk == 0
    qspec = pl.BlockSpec((None, tq, group * dh), lambda b, h, i: (b, i, h))
    kvspec = pl.BlockSpec((None, lk, dh), lambda b, h, i: (b, 0, h))
    return pl.pallas_call(
        functools.partial(_attn_kernel, group, tk),
        grid=(bn, nkv, lq // tq),
        in_specs=[qspec, kvspec, kvspec],
        out_specs=qspec,
        out_shape=jax.ShapeDtypeStruct((bn, lq, dq), BF16),
        scratch_shapes=[pltpu.VMEM((group, tq, 2 * dh), F32), pltpu.VMEM((group, tq, 1), F32)],
        compiler_params=_params(("parallel", "parallel", "parallel")),
        name="gqa_attn",
    )(q, k, v)


def _attn_fast_kernel(group, bound_ref, q_ref, k_ref, vt_ref, o_ref, qt_ref, acc_ref):
    dh = HEAD_DIM
    nk, _, tk = vt_ref.shape
    for g in range(group):
        qt_ref[g] = q_ref[:, g * dh:(g + 1) * dh].astype(F32).T.astype(BF16)
    acc_ref[...] = jnp.zeros_like(acc_ref)
    ones = jnp.ones((ONES_ROWS, tk), BF16)
    bound = bound_ref[0]

    def step(j, carry):
        vext = jnp.concatenate([vt_ref[j], ones], axis=0)
        base = pl.multiple_of(j * tk, tk)
        work = [(s, g) for s in range(tk // ATT_SUB) for g in range(group)]

        def scores(s, g):
            kc = k_ref[pl.ds(base + s * ATT_SUB, ATT_SUB), :]
            return jnp.dot(kc, qt_ref[g], preferred_element_type=F32)

        pending = [scores(*w) for w in work[:ATT_AHEAD]]
        tot = [None] * group
        for idx, (s, g) in enumerate(work):
            if idx + ATT_AHEAD < len(work):
                pending.append(scores(*work[idx + ATT_AHEAD]))
            pt = jnp.exp2(pending.pop(0) - bound).astype(BF16)
            pv = jnp.dot(vext[:, s * ATT_SUB:(s + 1) * ATT_SUB], pt, preferred_element_type=F32)
            tot[g] = pv if tot[g] is None else tot[g] + pv
        for g in range(group):
            acc_ref[g] += tot[g]
        return carry

    lax.fori_loop(0, nk, step, 0)
    for g in range(group):
        acc = acc_ref[g]
        o_ref[:, g * dh:(g + 1) * dh] = (acc[:dh] / acc[dh:dh + 1]).T.astype(BF16)


def _attn_fast_call(bound, q, k, v, tq, tk):
    bn, lq, dq = q.shape
    lk = k.shape[1]
    dh = HEAD_DIM
    nkv = k.shape[2] // dh
    group = dq // dh // nkv
    assert lq % tq == 0 and lk % tk == 0
    nk = lk // tk
    vt = v.reshape(bn, nk, tk, nkv, dh).transpose(0, 3, 1, 4, 2)
    qspec = pl.BlockSpec((None, tq, group * dh), lambda b, h, i: (b, i, h))
    return pl.pallas_call(
        functools.partial(_attn_fast_kernel, group),
        grid=(bn, nkv, lq // tq),
        in_specs=[pl.BlockSpec(memory_space=pltpu.SMEM), qspec,
                  pl.BlockSpec((None, lk, dh), lambda b, h, i: (b, 0, h)),
                  pl.BlockSpec((None, None, nk, dh, tk), lambda b, h, i: (b, h, 0, 0, 0))],
        out_specs=qspec,
        out_shape=jax.ShapeDtypeStruct((bn, lq, dq), BF16),
        scratch_shapes=[pltpu.VMEM((group, dh, tq), BF16), pltpu.VMEM((group, dh + ONES_ROWS, tq), F32)],
        compiler_params=_params(("parallel", "parallel", "parallel")),
        name="gqa_attn_fast",
    )(bound, q, k, vt)


def _score_bound(qg, kg):
    return (1.02 * HEAD_DIM * QSCALE) * jnp.max(jnp.abs(qg)) * jnp.max(jnp.abs(kg))


def _attention(q, k, v, qg, kg):
    lk = k.shape[1]
    tk_fast = max(t for t in range(ATT_SUB, min(lk, ATT_MAX_TRIP) + 1, ATT_SUB) if lk % t == 0)
    bound = _score_bound(qg, kg).reshape(1)
    return lax.cond(bound[0] <= SAFE_BOUND,
                    lambda: _attn_fast_call(bound, q, k, v, 256, tk_fast),
                    lambda: _attn_call(q, k, v, 256, _kv_chunk(lk)))


def _kv_chunk(lk):
    for tk in (768, 512, 384, 256, 128):
        if lk % tk == 0:
            return tk
    raise ValueError(f"key length {lk} is not a multiple of 128")


def _mix1_kernel(o_ref, w_ref, x_ref, g1_ref, ng1_ref, ng2_ref, sc2_ref, sh2_ref, x1_ref, h2_ref):
    _mixer_tail(o_ref[...], w_ref, x_ref, g1_ref, ng1_ref, ng2_ref, sc2_ref, sh2_ref, x1_ref, h2_ref)


def _mix1_call(o, w_out, x, g1, ng1, ng2, sc2, sh2):
    bn, ln, d = x.shape
    tm = _row_tile(ln)
    return pl.pallas_call(
        _mix1_kernel,
        grid=(bn, ln // tm),
        in_specs=[_rows(tm, o.shape[2]), _full(w_out.shape), _rows(tm, d), _perb(d), _full((1, d)),
                  _full((1, d)), _perb(d), _perb(d)],
        out_specs=[_rows(tm, d), _rows(tm, d)],
        out_shape=[jax.ShapeDtypeStruct((bn, ln, d), F32), jax.ShapeDtypeStruct((bn, ln, d), BF16)],
        compiler_params=_params(("parallel", "parallel")),
        name="mix1",
    )(o, w_out, x, g1, ng1, ng2, sc2, sh2)


def kernel(x, c, ctx, c_ctx, ada_w, ada_b, norm_g, ab_w_in, ab_w_out, pool_w, pool_scale, hg_lower, hg_onorm_g,
           att_w_in, att_w_out, att_qnorm_g, att_knorm_g, ffn_w_in, ffn_w_out):
    bn, ln, d = x.shape
    depth = ada_w.shape[0]
    hw = hg_lower.shape[2]
    pwid = pool_scale.shape[1]
    nh = hw // HEAD_DIM

    pad = (-(bn + 1)) % 8
    cc = jnp.concatenate([c, c_ctx[None, :], jnp.zeros((pad, d), F32)], axis=0)
    mods = _ada_call(cc, ada_w, ada_b)

    def split_mods(l, ctx_rows):
        m = mods[l, bn:bn + 1] if ctx_rows else mods[l, :bn]
        m = jnp.broadcast_to(m, (bn, 6 * d)).reshape(bn, 1, 6, d)
        return [m[:, :, t, :] for t in range(6)]

    lat, cx = x, ctx
    for l in range(depth):
        j = l // 2
        need_ctx = l < depth - 1
        ng = [norm_g[l, t][None, :] for t in range(4)]
        m_lat = split_mods(l, False)
        m_ctx = split_mods(l, True)
        w_ffn_in = ffn_w_in[l].astype(BF16)
        w_ffn_out = ffn_w_out[l].astype(BF16)

        if l % 2 == 0:
            w_in = ab_w_in[j].astype(BF16)
            w_out = ab_w_out[j].astype(BF16)
            pw = pool_w[j].astype(BF16)
            ps = pool_scale[j][None, :]
            og = hg_onorm_g[j][None, :]
            hgl = hg_lower.reshape(hg_lower.shape[0], 2 * hw)
            pc = _proj0_call(cx, m_ctx[1], m_ctx[0], ng[0], w_in, hgl, j, pwid, hw)
            plat = _proj0_call(lat, m_lat[1], m_lat[0], ng[0], w_in, hgl, j, pwid, hw)
            zero = jnp.zeros((bn, nh, HEAD_DIM, HEAD_DIM), F32)
            u_c, q_c, kf_c, kb_c, lff_c, lfb_c, v_c, g_c = pc
            u_l, q_l, kf_l, kb_l, lff_l, lfb_l, v_l, g_l = plat

            def scan(chunk, direct, tb):
                oc_f, oc_b, s_f, s_b = _hgrn_call(q_c, kf_c, kb_c, lff_c, lfb_c, v_c, zero, zero, chunk, direct, tb)
                ol_f, ol_b, _, _ = _hgrn_call(q_l, kf_l, kb_l, lff_l, lfb_l, v_l, s_f, s_b, chunk, direct, tb)
                return oc_f, oc_b, ol_f, ol_b

            lbound = jnp.cumsum(jax.nn.softmax(hg_lower.astype(F32), axis=0), axis=0)[j]
            worst = HG_CHUNK * jnp.max(-jnp.log(lbound))
            of_c, ob_c, of_l, ob_l = lax.cond(worst <= HG_SAFE_DECAY,
                                              lambda: scan(HG_CHUNK, False, 256),
                                              lambda: scan(HG_SAFE_CHUNK, True, 128))
            x1_l, h2_l = _mix0_call(u_l, of_l, ob_l, g_l, pw, ps, og, w_out, lat, m_lat[2], ng[1], ng[2],
                                    m_lat[4], m_lat[3])
            if need_ctx:
                x1_c, h2_c = _mix0_call(u_c, of_c, ob_c, g_c, pw, ps, og, w_out, cx, m_ctx[2], ng[1], ng[2],
                                        m_ctx[4], m_ctx[3])
        else:
            perm = _head_perm()
            n_qk = att_w_in.shape[2] // HEAD_DIM - ATT_KV_HEADS
            cols = np.concatenate([hd * HEAD_DIM + perm for hd in range(n_qk)]
                                  + [np.arange(n_qk * HEAD_DIM, att_w_in.shape[2])])
            w_in = att_w_in[j][:, cols].astype(BF16)
            w_out = att_w_out[j].astype(BF16)
            qg = att_qnorm_g[j][perm][None, :]
            kg = att_knorm_g[j][perm][None, :]
            outs_c = _proj1_call(cx, m_ctx[1], m_ctx[0], ng[0], w_in, qg, kg, need_ctx, False)
            q_l, k_l, v_l = _proj1_call(lat, m_lat[1], m_lat[0], ng[0], w_in, qg, kg, True, True)
            k_c, v_c = outs_c[-2], outs_c[-1]
            k_all = jnp.concatenate([k_l, k_c], axis=1)
            v_all = jnp.concatenate([v_l, v_c], axis=1)
            o_l = _attention(q_l, k_all, v_all, qg, kg)
            x1_l, h2_l = _mix1_call(o_l, w_out, lat, m_lat[2], ng[1], ng[2], m_lat[4], m_lat[3])
            if need_ctx:
                o_c = _attn_call(outs_c[0], k_c, v_c, 256, _kv_chunk(k_c.shape[1]))
                x1_c, h2_c = _mix1_call(o_c, w_out, cx, m_ctx[2], ng[1], ng[2], m_ctx[4], m_ctx[3])

        lat = _ffn_call(x1_l, h2_l, w_ffn_in, w_ffn_out, m_lat[5], ng[3])
        if need_ctx:
            cx = _ffn_call(x1_c, h2_c, w_ffn_in, w_ffn_out, m_ctx[5], ng[3])
    return lat
```

```python
import functools

import numpy as np
import jax
import jax.numpy as jnp
from jax import lax
from jax.experimental import pallas as pl
from jax.experimental.pallas import tpu as pltpu

F32 = jnp.float32
BF16 = jnp.bfloat16

RMS_EPS = 1e-6
GRID_W = 64
POOL_WINDOWS = (2, 4, 8, 16)
HEAD_DIM = 128
ATT_KV_HEADS = 2
ROPE_THETA = 10000.0
HG_CHUNK = 64
HG_SAFE_CHUNK = 16
HG_SAFE_DECAY = 80.0
TAIL_ROWS = 256
POOL_HALO = 16
LOG2E = 1.4426950408889634
QSCALE = HEAD_DIM ** -0.5 * LOG2E
ONES_ROWS = 16
ATT_SUB = 256
ATT_AHEAD = 3
SAFE_BOUND = 60.0

VMEM_LIMIT = 56 * 1024 * 1024

_NT = (((1,), (1,)), ((), ()))
_TN = (((0,), (0,)), ((), ()))


def _params(sem):
    return pltpu.CompilerParams(dimension_semantics=sem, vmem_limit_bytes=VMEM_LIMIT)


def _rows(tm, w):
    return pl.BlockSpec((None, tm, w), lambda b, i: (b, i, 0))


def _perb(w):
    return pl.BlockSpec((None, 1, w), lambda b, i: (b, 0, 0))


def _full(shape):
    nd = len(shape)
    return pl.BlockSpec(shape, lambda b, i: (0,) * nd, pipeline_mode=pl.Buffered(1))


def _sigmoid(z):
    return 0.5 * jnp.tanh(0.5 * z) + 0.5


def _silu(z):
    return z * _sigmoid(z)


def _rms(x, g):
    return x * lax.rsqrt(jnp.mean(x * x, axis=-1, keepdims=True) + RMS_EPS) * g


def _row_tile(n):
    return 512 if n % 512 == 0 else 256


def _ada_kernel(c_ref, w_ref, b_ref, o_ref):
    o_ref[...] = jnp.dot(_silu(c_ref[...]), w_ref[...], preferred_element_type=F32) + b_ref[...]


def _ada_call(cc, ada_w, ada_b):
    depth, d, n6 = ada_w.shape
    rows = cc.shape[0]
    tn = n6 // 4
    return pl.pallas_call(
        _ada_kernel,
        grid=(depth, n6 // tn),
        in_specs=[
            pl.BlockSpec((rows, d), lambda l, j: (0, 0)),
            pl.BlockSpec((None, d, tn), lambda l, j: (l, 0, j)),
            pl.BlockSpec((None, 1, tn), lambda l, j: (l, 0, j)),
        ],
        out_specs=pl.BlockSpec((None, rows, tn), lambda l, j: (l, 0, j)),
        out_shape=jax.ShapeDtypeStruct((depth, rows, n6), F32),
        compiler_params=_params(("arbitrary", "arbitrary")),
        name="ada_mod",
    )(cc, ada_w, ada_b.reshape(depth, 1, n6))


def _proj0_kernel(layer_j, pw, hw, x_ref, sc_ref, sh_ref, ng_ref, w_ref, hgl_ref,
                  u_ref, q_ref, kf_ref, kb_ref, lff_ref, lfb_ref, v_ref, g_ref):
    h = (_rms(x_ref[...], ng_ref[...]) * (1.0 + sc_ref[...]) + sh_ref[...]).astype(BF16)

    def col(lo, width):
        return jnp.dot(h, w_ref[:, lo:lo + width], preferred_element_type=F32)

    hgl = hgl_ref[...]
    e = jnp.exp(hgl - jnp.max(hgl, axis=0, keepdims=True))
    sm = e / jnp.sum(e, axis=0, keepdims=True)
    lbound = jnp.sum(sm[:layer_j + 1], axis=0, keepdims=True)

    def gate(d, k_ref, l_ref):
        def finish(z):
            lb = lbound[:, d * hw:(d + 1) * hw]
            f = lb + (1.0 - lb) * _sigmoid(z)
            k_ref[...] = (1.0 - f).astype(BF16)
            l_ref[...] = jnp.log(f)
        return finish

    def plain(o_ref, act):
        def finish(z):
            o_ref[...] = act(z).astype(BF16)
        return finish

    blocks = [(0, pw, plain(u_ref, lambda z: z)), (pw, hw, plain(q_ref, _silu)),
              (pw + hw, hw, gate(0, kf_ref, lff_ref)), (pw + 2 * hw, hw, gate(1, kb_ref, lfb_ref)),
              (pw + 3 * hw, hw, plain(v_ref, lambda z: z)), (pw + 4 * hw, hw, plain(g_ref, _silu))]
    z = col(*blocks[0][:2])
    for idx, (_, _, finish) in enumerate(blocks):
        z_next = col(*blocks[idx + 1][:2]) if idx + 1 < len(blocks) else None
        finish(z)
        z = z_next


def _proj0_call(x, sc, sh, ng, w, hgl, layer_j, pw, hw):
    bn, ln, d = x.shape
    tm = _row_tile(ln)
    bf = jax.ShapeDtypeStruct((bn, ln, hw), BF16)
    ff = jax.ShapeDtypeStruct((bn, ln, hw), F32)
    return pl.pallas_call(
        functools.partial(_proj0_kernel, layer_j, pw, hw),
        grid=(bn, ln // tm),
        in_specs=[_rows(tm, d), _perb(d), _perb(d), _full((1, d)), _full(w.shape), _full(hgl.shape)],
        out_specs=[_rows(tm, pw)] + [_rows(tm, hw)] * 7,
        out_shape=[jax.ShapeDtypeStruct((bn, ln, pw), BF16), bf, bf, bf, ff, ff, bf, bf],
        compiler_params=_params(("parallel", "parallel")),
        name="proj0",
    )(x, sc, sh, ng, w, hgl)


def _split3(x):
    hi = x.astype(BF16)
    r = x - hi.astype(F32)
    mid = r.astype(BF16)
    lo = (r - mid.astype(F32)).astype(BF16)
    return hi, mid, lo


def _direct_scores(q, k, b):
    c = q.shape[0]
    lane = lax.broadcasted_iota(jnp.int32, (c, c), 1)
    a = jnp.zeros((c, c), F32)
    for s in range(c):
        e = jnp.exp(jnp.minimum(b - b[s:s + 1, :], 0.0))
        col = jnp.sum(q * e * k[s:s + 1, :], axis=1, keepdims=True)
        a = jnp.where(lane == s, col, a)
    return a


def _hgrn_kernel(nh, c, direct, qf_ref, kf_ref, lf_ref, vf_ref, qb_ref, kb_ref, lb_ref, vb_ref, s0f_ref, s0b_ref,
                 of_ref, ob_ref, sff_ref, sfb_ref, stf, stb):
    i = pl.program_id(1)
    nch = qf_ref.shape[0] // c

    @pl.when(i == 0)
    def _():
        stf[...] = s0f_ref[...]
        stb[...] = s0b_ref[...]

    r = lax.broadcasted_iota(jnp.int32, (c, c), 0)
    s = lax.broadcasted_iota(jnp.int32, (c, c), 1)
    lower = s <= r
    upper = s >= r
    tri_f = jnp.where(lower, 1.0, 0.0).astype(BF16)
    tri_b = jnp.where(upper, 1.0, 0.0).astype(BF16)

    def cumul(tri, lf):
        hi, mid, lo = _split3(lf)
        return (jnp.dot(tri, hi, preferred_element_type=F32)
                + jnp.dot(tri, mid, preferred_element_type=F32)
                + jnp.dot(tri, lo, preferred_element_type=F32))

    dirs = ((False, qf_ref, kf_ref, lf_ref, vf_ref, of_ref, stf, tri_f, lower),
            (True, qb_ref, kb_ref, lb_ref, vb_ref, ob_ref, stb, tri_b, upper))

    def rows_of(rev, ci):
        return pl.ds((nch - 1 - ci if rev else ci) * c, c)

    units = [(ci, d, h) for ci in range(nch) for d in range(2) for h in range(nh)]

    decay = {}
    for ci in range(nch):
        for d, (rev, _, _, l_ref, _, _, _, tri, _) in enumerate(dirs):
            decay[ci, d] = cumul(tri, l_ref[rows_of(rev, ci), :])

    q1s, k2s, ebls, scores = {}, {}, {}, {}
    for u in units:
        ci, d, h = u
        rev, q_ref, k_ref = dirs[d][:3]
        rows, cols = rows_of(rev, ci), pl.ds(h * HEAD_DIM, HEAD_DIM)
        b = decay[ci, d][:, h * HEAD_DIM:(h + 1) * HEAD_DIM]
        last = 0 if rev else c - 1
        qf = q_ref[rows, cols].astype(F32)
        kf = k_ref[rows, cols].astype(F32)
        q1s[u] = (qf * jnp.exp(b)).astype(BF16)
        ebls[u] = jnp.exp(b[last:last + 1, :])
        if direct:
            k2s[u] = (kf * jnp.exp(b[last:last + 1, :] - b)).astype(BF16)
            scores[u] = _direct_scores(qf, kf, b)
        else:
            k1f = kf * jnp.exp(-b)
            k2s[u] = (k1f * ebls[u]).astype(BF16)
            scores[u] = lax.dot_general(q1s[u], k1f.astype(BF16), _NT, preferred_element_type=F32)

    state = {(d, h): dirs[d][6][h] for d in range(2) for h in range(nh)}
    for u in units:
        ci, d, h = u
        rev, v_ref, o_ref, mask = dirs[d][0], dirs[d][4], dirs[d][5], dirs[d][8]
        rows, cols = rows_of(rev, ci), pl.ds(h * HEAD_DIM, HEAD_DIM)
        v = v_ref[rows, cols]
        a = jnp.where(mask, scores[u], 0.0).astype(BF16)
        st = state[d, h]
        o = jnp.dot(jnp.concatenate([q1s[u], a], axis=1), jnp.concatenate([st.astype(BF16), v], axis=0),
                    preferred_element_type=F32)
        o_ref[rows, cols] = o.astype(BF16)
        decay_col = jnp.broadcast_to(ebls[u], (HEAD_DIM, HEAD_DIM)).T
        state[d, h] = st * decay_col + lax.dot_general(k2s[u], v, _TN, preferred_element_type=F32)
    for (d, h), st in state.items():
        dirs[d][6][h] = st

    @pl.when(i == pl.num_programs(1) - 1)
    def _():
        sff_ref[...] = stf[...]
        sfb_ref[...] = stb[...]


def _hgrn_call(q, kf, kb, lff, lfb, v, s0f, s0b, chunk, direct, tb):
    bn, ln, hw = q.shape
    nh = hw // HEAD_DIM
    assert ln % tb == 0 and tb % chunk == 0
    n = ln // tb
    fwd = pl.BlockSpec((None, tb, hw), lambda b, i: (b, i, 0))
    bwd = pl.BlockSpec((None, tb, hw), lambda b, i: (b, n - 1 - i, 0))
    st = pl.BlockSpec((None, nh, HEAD_DIM, HEAD_DIM), lambda b, i: (b, 0, 0, 0))
    st_shape = jax.ShapeDtypeStruct((bn, nh, HEAD_DIM, HEAD_DIM), F32)
    o_shape = jax.ShapeDtypeStruct((bn, ln, hw), BF16)
    return pl.pallas_call(
        functools.partial(_hgrn_kernel, nh, chunk, direct),
        grid=(bn, n),
        in_specs=[fwd, fwd, fwd, fwd, bwd, bwd, bwd, bwd, st, st],
        out_specs=[fwd, bwd, st, st],
        out_shape=[o_shape, o_shape, st_shape, st_shape],
        scratch_shapes=[pltpu.VMEM((nh, HEAD_DIM, HEAD_DIM), F32)] * 2,
        compiler_params=_params(("parallel", "arbitrary")),
        name="hgrn_scan",
    )(q, kf, lff, v, q, kb, lfb, v, s0f, s0b)


def _mixer_tail(mix, w_ref, x_ref, g1_ref, ng1_ref, x1_ref):
    tm = mix.shape[0]
    rb = min(tm, TAIL_ROWS)
    gain1 = g1_ref[...] * ng1_ref[...]
    for r in range(tm // rb):
        rows = pl.ds(r * rb, rb)
        y = jnp.dot(mix[r * rb:(r + 1) * rb], w_ref[...], preferred_element_type=F32)
        x1_ref[rows, :] = x_ref[rows, :] + _rms(y, gain1)


def _mix0_kernel(seq_len, u_ref, up_ref, un_ref, of_ref, ob_ref, g_ref, pw_ref, ps_ref, og_ref, w_ref,
                 x_ref, g1_ref, ng1_ref, x1_ref, ubuf):
    i = pl.program_id(1)
    tm, pwid = u_ref.shape
    hal = POOL_HALO
    pg = pwid // len(POOL_WINDOWS)

    first = i == 0
    last = i == pl.num_programs(1) - 1
    ubuf[0:hal, :] = jnp.where(first, 0.0, up_ref[...].astype(F32))
    ubuf[hal:hal + tm, :] = u_ref[...].astype(F32)
    ubuf[hal + tm:, :] = jnp.where(last, 0.0, un_ref[...].astype(F32))

    edge = max(POOL_WINDOWS) // 2
    t_top = i * tm + lax.broadcasted_iota(jnp.int32, (edge, pg), 0)
    t_bot = t_top + (tm - edge)

    def clipped_mean(acc_rows, t, w):
        lo = jnp.maximum(t - w // 2, 0)
        hi = jnp.minimum(t - w // 2 + w, seq_len)
        return acc_rows / (hi - lo).astype(F32)

    parts = []
    for gi, w in enumerate(POOL_WINDOWS):
        cols = pl.ds(gi * pg, pg)
        acc = ubuf[pl.ds(hal - w // 2, tm), cols]
        for jj in range(1, w):
            acc = acc + ubuf[pl.ds(hal - w // 2 + jj, tm), cols]
        mean = jnp.concatenate([clipped_mean(acc[:edge], t_top, w), acc[edge:tm - edge] * (1.0 / w),
                                clipped_mean(acc[tm - edge:], t_bot, w)], axis=0)
        y = mean - ubuf[pl.ds(hal, tm), cols]
        y = jnp.dot(y.astype(BF16), pw_ref[gi], preferred_element_type=F32)
        parts.append((y * ps_ref[:, gi * pg:(gi + 1) * pg]).astype(BF16))

    osum = of_ref[...].astype(F32) + ob_ref[...].astype(F32)
    gate = g_ref[...].astype(F32)
    for h in range(osum.shape[1] // HEAD_DIM):
        sl = slice(h * HEAD_DIM, (h + 1) * HEAD_DIM)
        parts.append((_rms(osum[:, sl], og_ref[...]) * gate[:, sl]).astype(BF16))

    _mixer_tail(jnp.concatenate(parts, axis=-1), w_ref, x_ref, g1_ref, ng1_ref, x1_ref)


def _mix0_call(u, of, ob, g, pool_w, pool_scale, og, w_out, x, g1, ng1):
    bn, ln, d = x.shape
    pwid, hw = u.shape[2], g.shape[2]
    tm = _row_tile(ln)
    hb = tm // POOL_HALO
    nhb = ln // POOL_HALO
    prev = pl.BlockSpec((None, POOL_HALO, pwid), lambda b, i: (b, jnp.maximum(i * hb - 1, 0), 0))
    nxt = pl.BlockSpec((None, POOL_HALO, pwid), lambda b, i: (b, jnp.minimum((i + 1) * hb, nhb - 1), 0))
    return pl.pallas_call(
        functools.partial(_mix0_kernel, ln),
        grid=(bn, ln // tm),
        in_specs=[_rows(tm, pwid), prev, nxt, _rows(tm, hw), _rows(tm, hw), _rows(tm, hw),
                  _full(pool_w.shape), _full((1, pwid)), _full((1, HEAD_DIM)), _full(w_out.shape),
                  _rows(tm, d), _perb(d), _full((1, d))],
        out_specs=_rows(tm, d),
        out_shape=jax.ShapeDtypeStruct((bn, ln, d), F32),
        scratch_shapes=[pltpu.VMEM((tm + 2 * POOL_HALO, pwid), F32)],
        compiler_params=_params(("parallel", "parallel")),
        name="mix0",
    )(u, u, u, of, ob, g, pool_w, pool_scale, og, w_out, x, g1, ng1)


def _ffn_kernel(hc, x1_ref, ng2_ref, sc2_ref, sh2_ref, wi_ref, wo_ref, g2_ref, ng3_ref, o_ref, t_ref):
    hid = wo_ref.shape[0]
    tm = x1_ref.shape[0]
    rb = min(tm, TAIL_ROWS)
    gain2 = ng2_ref[...] * (1.0 + sc2_ref[...])
    for r in range(tm // rb):
        rows = pl.ds(r * rb, rb)
        h2 = (_rms(x1_ref[rows, :], gain2) + sh2_ref[...]).astype(BF16)
        for j in range(hid // hc):
            a = jnp.dot(h2, wi_ref[:, j * hc:(j + 1) * hc], preferred_element_type=F32)
            b = jnp.dot(h2, wi_ref[:, hid + j * hc:hid + (j + 1) * hc], preferred_element_type=F32)
            t_ref[rows, j * hc:(j + 1) * hc] = (_silu(a) * b).astype(BF16)
    f = jnp.dot(t_ref[...], wo_ref[...], preferred_element_type=F32)
    o_ref[...] = x1_ref[...] + g2_ref[...] * _rms(f, ng3_ref[...])


def _layer_block(shape, layer):
    nd = len(shape)
    return pl.BlockSpec((None,) + tuple(shape[1:]), lambda b, i: (layer,) + (0,) * (nd - 1),
                        pipeline_mode=pl.Buffered(1))


def _ffn_call(x1, ng2, sc2, sh2, w_in, w_out, layer, g2, ng3):
    bn, ln, d = x1.shape
    hid = w_out.shape[1]
    tm = _row_tile(ln)
    hc = 256
    assert hid % hc == 0
    return pl.pallas_call(
        functools.partial(_ffn_kernel, hc),
        grid=(bn, ln // tm),
        in_specs=[_rows(tm, d), _full((1, d)), _perb(d), _perb(d), _layer_block(w_in.shape, layer),
                  _layer_block(w_out.shape, layer), _perb(d), _full((1, d))],
        out_specs=_rows(tm, d),
        out_shape=jax.ShapeDtypeStruct((bn, ln, d), F32),
        scratch_shapes=[pltpu.VMEM((tm, hid), BF16)],
        compiler_params=_params(("parallel", "parallel")),
        name="ffn",
    )(x1, ng2, sc2, sh2, w_in, w_out, g2, ng3)


def _head_perm():
    n_freq = HEAD_DIM // 4
    blocks = np.arange(HEAD_DIM).reshape(2, 2, n_freq)
    return blocks.transpose(1, 0, 2).reshape(-1)


def _rope_tables(ln):
    half = HEAD_DIM // 2
    n_freq = HEAD_DIM // 4
    pos = np.arange(ln)
    inv = (ROPE_THETA ** (-np.arange(n_freq, dtype=np.float32) / n_freq)).astype(np.float32)
    lane = _head_perm()
    p = np.where(lane[None, :] < half, (pos // GRID_W)[:, None], (pos % GRID_W)[:, None]).astype(np.float32)
    ang = (p * inv[lane % n_freq][None, :]).astype(np.float32).astype(np.float64)
    sign = np.where(np.arange(HEAD_DIM) < half, -1.0, 1.0)
    return jnp.asarray(np.cos(ang), F32), jnp.asarray(np.sin(ang) * sign[None, :], F32)


def _gain_tables(g, rope):
    if rope is None:
        return jnp.concatenate([g, g], axis=1), None
    cos, sin = rope
    g_rolled = pltpu.roll(jnp.broadcast_to(g, (8, HEAD_DIM)), HEAD_DIM // 2, 1)[:1]
    gcos, gsin = cos * g, sin * g_rolled
    return jnp.concatenate([gcos, gcos], axis=1), jnp.concatenate([gsin, gsin], axis=1)


def _pair_matrices():
    n = 2 * HEAD_DIM
    r = lax.broadcasted_iota(jnp.int32, (n, n), 0)
    c = lax.broadcasted_iota(jnp.int32, (n, n), 1)
    same_head = (r // HEAD_DIM) == (c // HEAD_DIM)
    ones = jnp.where(same_head, 1.0, 0.0).astype(BF16)
    swap = jnp.where(same_head & ((r % HEAD_DIM) == ((c + HEAD_DIM // 2) % HEAD_DIM)), 1.0, 0.0).astype(BF16)
    return ones, swap


def _pair_stats(z, with_swap, ones, swap):
    ssq = jnp.dot((z * z).astype(BF16), ones, preferred_element_type=F32)
    swapped = jnp.dot(z.astype(BF16), swap, preferred_element_type=F32) if with_swap else None
    return ssq, swapped


def _pair_finish(z, stats, tables):
    gcos, gsin = tables
    ssq, swapped = stats
    y = z * gcos
    if swapped is not None:
        y = y + swapped * gsin
    return (y * lax.rsqrt(ssq * (1.0 / HEAD_DIM) + RMS_EPS)).astype(BF16)


def _proj1_kernel(with_q, with_rope, nq, nkv, *refs):
    x_ref, sc_ref, sh_ref, ng_ref, w_ref, qg_ref, kg_ref = refs[:7]
    refs = refs[7:]
    rope = None
    if with_rope:
        rope = (refs[0][...], refs[1][...])
        refs = refs[2:]
    h = (_rms(x_ref[...], ng_ref[...]) * (1.0 + sc_ref[...]) + sh_ref[...]).astype(BF16)
    dh = HEAD_DIM

    first = 0 if with_q else nq
    z = jnp.dot(h, w_ref[:, first * dh:], preferred_element_type=F32)

    def pair(hd):
        return z[:, (hd - first) * dh:(hd - first + 2) * dh]

    ones, swap = _pair_matrices()
    if with_q:
        q_ref, k_ref, v_ref = refs
        q_tables = _gain_tables(qg_ref[...] * QSCALE, rope)
    else:
        k_ref, v_ref = refs
    k_tables = _gain_tables(kg_ref[...], rope)
    todo = [(hq, q_ref, hq, q_tables) for hq in range(0, nq if with_q else 0, 2)]
    todo += [(nq + hk, k_ref, hk, k_tables) for hk in range(0, nkv, 2)]
    stats = [_pair_stats(pair(hd), with_rope, ones, swap) for hd, _, _, _ in todo]
    for (hd, o_ref, ho, tables), st in zip(todo, stats):
        o_ref[:, ho * dh:(ho + 2) * dh] = _pair_finish(pair(hd), st, tables)
    for hk in range(0, nkv, 2):
        v_ref[hk * dh:(hk + 2) * dh, :] = pair(nq + nkv + hk).T.astype(BF16)


def _proj1_call(x, sc, sh, ng, w, qg, kg, with_q, with_rope):
    bn, ln, d = x.shape
    dh = HEAD_DIM
    nkv = ATT_KV_HEADS
    nq = w.shape[1] // dh - 2 * nkv
    tm = _row_tile(ln)
    in_specs = [_rows(tm, d), _perb(d), _perb(d), _full((1, d)), _full(w.shape), _full((1, dh)), _full((1, dh))]
    args = [x, sc, sh, ng, w, qg, kg]
    if with_rope:
        tab = pl.BlockSpec((tm, dh), lambda b, i: (i, 0))
        in_specs += [tab, tab]
        args += list(_rope_tables(ln))
    out_specs = [_rows(tm, nkv * dh), pl.BlockSpec((None, nkv * dh, tm), lambda b, i: (b, 0, i))]
    out_shape = [jax.ShapeDtypeStruct((bn, ln, nkv * dh), BF16), jax.ShapeDtypeStruct((bn, nkv * dh, ln), BF16)]
    if with_q:
        out_specs = [_rows(tm, nq * dh)] + out_specs
        out_shape = [jax.ShapeDtypeStruct((bn, ln, nq * dh), BF16)] + out_shape
    return pl.pallas_call(
        functools.partial(_proj1_kernel, with_q, with_rope, nq, nkv),
        grid=(bn, ln // tm),
        in_specs=in_specs,
        out_specs=out_specs,
        out_shape=out_shape,
        compiler_params=_params(("parallel", "parallel")),
        name="proj1_q" if with_q else "proj1_kv",
    )(*args)


def _attn_kernel(group, tk, q_ref, k_ref, v_ref, o_ref, acc_ref, m_ref):
    dh = HEAD_DIM
    tq = q_ref.shape[0]
    nk = k_ref.shape[0] // tk
    acc_ref[...] = jnp.zeros_like(acc_ref)
    m_ref[...] = jnp.full_like(m_ref, -jnp.inf)
    ones = jnp.ones((tk, dh), BF16)

    def step(j, carry):
        rows = pl.ds(pl.multiple_of(j * tk, tk), tk)
        kc = k_ref[rows, :]
        vc = jnp.concatenate([v_ref[rows, :], ones], axis=-1)
        for g in range(group):
            qg = q_ref[:, g * dh:(g + 1) * dh]
            s = lax.dot_general(qg, kc, _NT, preferred_element_type=F32)
            m_old = m_ref[g]
            m_new = jnp.maximum(m_old, jnp.max(s, axis=1, keepdims=True))
            p = jnp.exp2(s - m_new).astype(BF16)
            alpha = jnp.exp2(m_old - m_new)
            acc_ref[g] = alpha * acc_ref[g] + jnp.dot(p, vc, preferred_element_type=F32)
            m_ref[g] = m_new
        return carry

    lax.fori_loop(0, nk, step, 0)
    for g in range(group):
        acc = acc_ref[g]
        o_ref[:, g * dh:(g + 1) * dh] = (acc[:, :dh] / acc[:, dh:dh + 1]).astype(BF16)


def _attn_call(q, k, v, tq, tk):
    bn, lq, dq = q.shape
    lk = k.shape[1]
    dh = HEAD_DIM
    nkv = k.shape[2] // dh
    group = dq // dh // nkv
    assert lq % tq == 0 and lk % tk == 0
    qspec = pl.BlockSpec((None, tq, group * dh), lambda b, h, i: (b, i, h))
    kvspec = pl.BlockSpec((None, lk, dh), lambda b, h, i: (b, 0, h))
    return pl.pallas_call(
        functools.partial(_attn_kernel, group, tk),
        grid=(bn, nkv, lq // tq),
        in_specs=[qspec, kvspec, kvspec],
        out_specs=qspec,
        out_shape=jax.ShapeDtypeStruct((bn, lq, dq), BF16),
        scratch_shapes=[pltpu.VMEM((group, tq, 2 * dh), F32), pltpu.VMEM((group, tq, 1), F32)],
        compiler_params=_params(("parallel", "parallel", "parallel")),
        name="gqa_attn",
    )(q, k, v)


def _attn_fast_kernel(group, bound_ref, q_ref, k_ref, vt_ref, o_ref, qt_ref):
    dh = HEAD_DIM
    lk = k_ref.shape[0]
    for g in range(group):
        qt_ref[g] = q_ref[:, g * dh:(g + 1) * dh].astype(F32).T.astype(BF16)
    ones = jnp.ones((ONES_ROWS, ATT_SUB), BF16)
    bound = bound_ref[0]
    work = [(s, g) for s in range(lk // ATT_SUB) for g in range(group)]

    def scores(s, g):
        kc = k_ref[s * ATT_SUB:(s + 1) * ATT_SUB, :]
        return jnp.dot(kc, qt_ref[g], preferred_element_type=F32)

    pending = [scores(*w) for w in work[:ATT_AHEAD]]
    tot = [None] * group
    for idx, (s, g) in enumerate(work):
        if idx + ATT_AHEAD < len(work):
            pending.append(scores(*work[idx + ATT_AHEAD]))
        pt = jnp.exp2(pending.pop(0) - bound).astype(BF16)
        vext = jnp.concatenate([vt_ref[:, s * ATT_SUB:(s + 1) * ATT_SUB], ones], axis=0)
        pv = jnp.dot(vext, pt, preferred_element_type=F32)
        tot[g] = pv if tot[g] is None else tot[g] + pv
    for g in range(group):
        o_ref[:, g * dh:(g + 1) * dh] = (tot[g][:dh] / tot[g][dh:dh + 1]).T.astype(BF16)


def _attn_fast_call(bound, q, k, vt, tq):
    bn, lq, dq = q.shape
    lk = k.shape[1]
    dh = HEAD_DIM
    nkv = k.shape[2] // dh
    group = dq // dh // nkv
    assert lq % tq == 0 and lk % ATT_SUB == 0
    qspec = pl.BlockSpec((None, tq, group * dh), lambda b, h, i: (b, i, h))
    return pl.pallas_call(
        functools.partial(_attn_fast_kernel, group),
        grid=(bn, nkv, lq // tq),
        in_specs=[pl.BlockSpec(memory_space=pltpu.SMEM), qspec,
                  pl.BlockSpec((None, lk, dh), lambda b, h, i: (b, 0, h)),
                  pl.BlockSpec((None, dh, lk), lambda b, h, i: (b, h, 0))],
        out_specs=qspec,
        out_shape=jax.ShapeDtypeStruct((bn, lq, dq), BF16),
        scratch_shapes=[pltpu.VMEM((group, dh, tq), BF16)],
        compiler_params=_params(("parallel", "parallel", "parallel")),
        name="gqa_attn_fast",
    )(bound, q, k, vt)


def _score_bound(qg, kg):
    return (1.02 * HEAD_DIM * QSCALE) * jnp.max(jnp.abs(qg)) * jnp.max(jnp.abs(kg))


def _attention(q, k, vt, qg, kg):
    bound = _score_bound(qg, kg).reshape(1)
    return lax.cond(bound[0] <= SAFE_BOUND,
                    lambda: _attn_fast_call(bound, q, k, vt, 256),
                    lambda: _attn_call(q, k, jnp.swapaxes(vt, 1, 2), 256, _kv_chunk(k.shape[1])))


def _kv_chunk(lk):
    for tk in (768, 512, 384, 256, 128):
        if lk % tk == 0:
            return tk
    raise ValueError(f"key length {lk} is not a multiple of 128")


def _mix1_kernel(o_ref, w_ref, x_ref, g1_ref, ng1_ref, x1_ref):
    _mixer_tail(o_ref[...], w_ref, x_ref, g1_ref, ng1_ref, x1_ref)


def _mix1_call(o, w_out, x, g1, ng1):
    bn, ln, d = x.shape
    tm = _row_tile(ln)
    return pl.pallas_call(
        _mix1_kernel,
        grid=(bn, ln // tm),
        in_specs=[_rows(tm, o.shape[2]), _full(w_out.shape), _rows(tm, d), _perb(d), _full((1, d))],
        out_specs=_rows(tm, d),
        out_shape=jax.ShapeDtypeStruct((bn, ln, d), F32),
        compiler_params=_params(("parallel", "parallel")),
        name="mix1",
    )(o, w_out, x, g1, ng1)


def kernel(x, c, ctx, c_ctx, ada_w, ada_b, norm_g, ab_w_in, ab_w_out, pool_w, pool_scale, hg_lower, hg_onorm_g,
           att_w_in, att_w_out, att_qnorm_g, att_knorm_g, ffn_w_in, ffn_w_out):
    bn, ln, d = x.shape
    depth = ada_w.shape[0]
    hw = hg_lower.shape[2]
    pwid = pool_scale.shape[1]
    nh = hw // HEAD_DIM

    pad = (-(bn + 1)) % 8
    cc = jnp.concatenate([c, c_ctx[None, :], jnp.zeros((pad, d), F32)], axis=0)
    mods = _ada_call(cc, ada_w, ada_b)

    def split_mods(l, ctx_rows):
        m = mods[l, bn:bn + 1] if ctx_rows else mods[l, :bn]
        m = jnp.broadcast_to(m, (bn, 6 * d)).reshape(bn, 1, 6, d)
        return [m[:, :, t, :] for t in range(6)]

    w_ffn_in = ffn_w_in.astype(BF16)
    w_ffn_out = ffn_w_out.astype(BF16)

    lat, cx = x, ctx
    for l in range(depth):
        j = l // 2
        need_ctx = l < depth - 1
        ng = [norm_g[l, t][None, :] for t in range(4)]
        m_lat = split_mods(l, False)
        m_ctx = split_mods(l, True)

        if l % 2 == 0:
            w_in = ab_w_in[j].astype(BF16)
            w_out = ab_w_out[j].astype(BF16)
            pw = pool_w[j].astype(BF16)
            ps = pool_scale[j][None, :]
            og = hg_onorm_g[j][None, :]
            hgl = hg_lower.reshape(hg_lower.shape[0], 2 * hw)
            pc = _proj0_call(cx, m_ctx[1], m_ctx[0], ng[0], w_in, hgl, j, pwid, hw)
            plat = _proj0_call(lat, m_lat[1], m_lat[0], ng[0], w_in, hgl, j, pwid, hw)
            zero = jnp.zeros((bn, nh, HEAD_DIM, HEAD_DIM), F32)
            u_c, q_c, kf_c, kb_c, lff_c, lfb_c, v_c, g_c = pc
            u_l, q_l, kf_l, kb_l, lff_l, lfb_l, v_l, g_l = plat

            def scan(chunk, direct, tb):
                oc_f, oc_b, s_f, s_b = _hgrn_call(q_c, kf_c, kb_c, lff_c, lfb_c, v_c, zero, zero, chunk, direct, tb)
                ol_f, ol_b, _, _ = _hgrn_call(q_l, kf_l, kb_l, lff_l, lfb_l, v_l, s_f, s_b, chunk, direct, tb)
                return oc_f, oc_b, ol_f, ol_b

            lbound = jnp.cumsum(jax.nn.softmax(hg_lower.astype(F32), axis=0), axis=0)[j]
            worst = HG_CHUNK * jnp.max(-jnp.log(lbound))
            of_c, ob_c, of_l, ob_l = lax.cond(worst <= HG_SAFE_DECAY,
                                              lambda: scan(HG_CHUNK, False, 256),
                                              lambda: scan(HG_SAFE_CHUNK, True, 128))
            x1_l = _mix0_call(u_l, of_l, ob_l, g_l, pw, ps, og, w_out, lat, m_lat[2], ng[1])
            if need_ctx:
                x1_c = _mix0_call(u_c, of_c, ob_c, g_c, pw, ps, og, w_out, cx, m_ctx[2], ng[1])
        else:
            perm = _head_perm()
            n_qk = att_w_in.shape[2] // HEAD_DIM - ATT_KV_HEADS
            cols = np.concatenate([hd * HEAD_DIM + perm for hd in range(n_qk)]
                                  + [np.arange(n_qk * HEAD_DIM, att_w_in.shape[2])])
            w_in = att_w_in[j][:, cols].astype(BF16)
            w_out = att_w_out[j].astype(BF16)
            qg = att_qnorm_g[j][perm][None, :]
            kg = att_knorm_g[j][perm][None, :]
            outs_c = _proj1_call(cx, m_ctx[1], m_ctx[0], ng[0], w_in, qg, kg, need_ctx, False)
            q_l, k_l, vt_l = _proj1_call(lat, m_lat[1], m_lat[0], ng[0], w_in, qg, kg, True, True)
            k_c, vt_c = outs_c[-2], outs_c[-1]
            k_all = jnp.concatenate([k_l, k_c], axis=1)
            vt_all = jnp.concatenate([vt_l, vt_c], axis=2)
            o_l = _attention(q_l, k_all, vt_all, qg, kg)
            x1_l = _mix1_call(o_l, w_out, lat, m_lat[2], ng[1])
            if need_ctx:
                o_c = _attn_call(outs_c[0], k_c, jnp.swapaxes(vt_c, 1, 2), 256, _kv_chunk(k_c.shape[1]))
                x1_c = _mix1_call(o_c, w_out, cx, m_ctx[2], ng[1])

        lat = _ffn_call(x1_l, ng[2], m_lat[4], m_lat[3], w_ffn_in, w_ffn_out, l, m_lat[5], ng[3])
        if need_ctx:
            cx = _ffn_call(x1_c, ng[2], m_ctx[4], m_ctx[3], w_ffn_in, w_ffn_out, l, m_ctx[5], ng[3])
    return lat
```

```python
import functools

import numpy as np
import jax
import jax.numpy as jnp
from jax import lax
from jax.experimental import pallas as pl
from jax.experimental.pallas import tpu as pltpu

F32 = jnp.float32
BF16 = jnp.bfloat16

RMS_EPS = 1e-6
GRID_W = 64
POOL_WINDOWS = (2, 4, 8, 16)
HEAD_DIM = 128
ATT_KV_HEADS = 2
ROPE_THETA = 10000.0
HG_CHUNK = 64
HG_SAFE_CHUNK = 16
HG_SAFE_DECAY = 80.0
TAIL_ROWS = 256
POOL_HALO = 16
LOG2E = 1.4426950408889634
QSCALE = HEAD_DIM ** -0.5 * LOG2E
ONES_ROWS = 16
ATT_TQ = 256
ATT_FAST_ROWS = 1024
ATT_SUB = 256
ATT_AHEAD = 3
SAFE_BOUND = 60.0

VMEM_LIMIT = 56 * 1024 * 1024

_NT = (((1,), (1,)), ((), ()))
_TN = (((0,), (0,)), ((), ()))


def _params(sem):
    return pltpu.CompilerParams(dimension_semantics=sem, vmem_limit_bytes=VMEM_LIMIT)


def _rows(tm, w):
    return pl.BlockSpec((None, tm, w), lambda b, i: (b, i, 0))


def _perb(w):
    return pl.BlockSpec((None, 1, w), lambda b, i: (b, 0, 0))


def _full(shape):
    nd = len(shape)
    return pl.BlockSpec(shape, lambda b, i: (0,) * nd, pipeline_mode=pl.Buffered(1))


def _sigmoid(z):
    return 0.5 * jnp.tanh(0.5 * z) + 0.5


def _silu(z):
    return z * _sigmoid(z)


def _rms(x, g):
    return x * lax.rsqrt(jnp.mean(x * x, axis=-1, keepdims=True) + RMS_EPS) * g


def _row_tile(n):
    return 512 if n % 512 == 0 else 256


def _ada_kernel(c_ref, w_ref, b_ref, o_ref):
    o_ref[...] = jnp.dot(_silu(c_ref[...]), w_ref[...], preferred_element_type=F32) + b_ref[...]


def _ada_call(cc, ada_w, ada_b):
    depth, d, n6 = ada_w.shape
    rows = cc.shape[0]
    tn = n6 // 4
    return pl.pallas_call(
        _ada_kernel,
        grid=(depth, n6 // tn),
        in_specs=[
            pl.BlockSpec((rows, d), lambda l, j: (0, 0)),
            pl.BlockSpec((None, d, tn), lambda l, j: (l, 0, j)),
            pl.BlockSpec((None, 1, tn), lambda l, j: (l, 0, j)),
        ],
        out_specs=pl.BlockSpec((None, rows, tn), lambda l, j: (l, 0, j)),
        out_shape=jax.ShapeDtypeStruct((depth, rows, n6), F32),
        compiler_params=_params(("arbitrary", "arbitrary")),
        name="ada_mod",
    )(cc, ada_w, ada_b.reshape(depth, 1, n6))


def _proj0_kernel(layer_j, pw, hw, x_ref, sc_ref, sh_ref, ng_ref, w_ref, hgl_ref,
                  u_ref, q_ref, kf_ref, kb_ref, lff_ref, lfb_ref, v_ref, g_ref):
    hgl = hgl_ref[...]
    e = jnp.exp(hgl - jnp.max(hgl, axis=0, keepdims=True))
    sm = e / jnp.sum(e, axis=0, keepdims=True)
    lbound = jnp.sum(sm[:layer_j + 1], axis=0, keepdims=True)

    tm = x_ref.shape[0]
    rb = min(tm, TAIL_ROWS)
    gain = ng_ref[...] * (1.0 + sc_ref[...])
    for r in range(tm // rb):
        rows = pl.ds(r * rb, rb)
        h = (_rms(x_ref[rows, :], gain) + sh_ref[...]).astype(BF16)

        def col(lo, width):
            return jnp.dot(h, w_ref[:, lo:lo + width], preferred_element_type=F32)

        u_ref[rows, :] = col(0, pw).astype(BF16)
        q_ref[rows, :] = _silu(col(pw, hw)).astype(BF16)
        for d, (k_ref, l_ref) in enumerate(((kf_ref, lff_ref), (kb_ref, lfb_ref))):
            lb = lbound[:, d * hw:(d + 1) * hw]
            f = lb + (1.0 - lb) * _sigmoid(col(pw + (1 + d) * hw, hw))
            k_ref[rows, :] = (1.0 - f).astype(BF16)
            l_ref[rows, :] = jnp.log(f)
        v_ref[rows, :] = col(pw + 3 * hw, hw).astype(BF16)
        g_ref[rows, :] = _silu(col(pw + 4 * hw, hw)).astype(BF16)


def _proj0_call(x, sc, sh, ng, w, hgl, layer_j, pw, hw):
    bn, ln, d = x.shape
    tm = _row_tile(ln)
    bf = jax.ShapeDtypeStruct((bn, ln, hw), BF16)
    ff = jax.ShapeDtypeStruct((bn, ln, hw), F32)
    return pl.pallas_call(
        functools.partial(_proj0_kernel, layer_j, pw, hw),
        grid=(bn, ln // tm),
        in_specs=[_rows(tm, d), _perb(d), _perb(d), _full((1, d)), _full(w.shape), _full(hgl.shape)],
        out_specs=[_rows(tm, pw)] + [_rows(tm, hw)] * 7,
        out_shape=[jax.ShapeDtypeStruct((bn, ln, pw), BF16), bf, bf, bf, ff, ff, bf, bf],
        compiler_params=_params(("parallel", "parallel")),
        name="proj0",
    )(x, sc, sh, ng, w, hgl)


def _split3(x):
    hi = x.astype(BF16)
    r = x - hi.astype(F32)
    mid = r.astype(BF16)
    lo = (r - mid.astype(F32)).astype(BF16)
    return hi, mid, lo


def _direct_scores(q, k, b):
    c = q.shape[0]
    lane = lax.broadcasted_iota(jnp.int32, (c, c), 1)
    a = jnp.zeros((c, c), F32)
    for s in range(c):
        e = jnp.exp(jnp.minimum(b - b[s:s + 1, :], 0.0))
        col = jnp.sum(q * e * k[s:s + 1, :], axis=1, keepdims=True)
        a = jnp.where(lane == s, col, a)
    return a


def _hgrn_kernel(nh, c, direct, qf_ref, kf_ref, lf_ref, vf_ref, qb_ref, kb_ref, lb_ref, vb_ref, s0f_ref, s0b_ref,
                 of_ref, ob_ref, sff_ref, sfb_ref, stf, stb):
    i = pl.program_id(1)
    nch = qf_ref.shape[0] // c

    @pl.when(i == 0)
    def _():
        stf[...] = s0f_ref[...]
        stb[...] = s0b_ref[...]

    r = lax.broadcasted_iota(jnp.int32, (c, c), 0)
    s = lax.broadcasted_iota(jnp.int32, (c, c), 1)
    lower = s <= r
    upper = s >= r
    tri_f = jnp.where(lower, 1.0, 0.0).astype(BF16)
    tri_b = jnp.where(upper, 1.0, 0.0).astype(BF16)

    def cumul(tri, lf):
        return jnp.dot(jnp.concatenate([tri, tri, tri], axis=1), jnp.concatenate(_split3(lf), axis=0),
                       preferred_element_type=F32)

    dirs = ((False, qf_ref, kf_ref, lf_ref, vf_ref, of_ref, stf, tri_f, lower),
            (True, qb_ref, kb_ref, lb_ref, vb_ref, ob_ref, stb, tri_b, upper))

    def rows_of(rev, ci):
        return pl.ds((nch - 1 - ci if rev else ci) * c, c)

    q1s, k2s, ebls, scores = {}, {}, {}, {}

    decay = {(ci, d): cumul(dirs[d][7], dirs[d][3][rows_of(dirs[d][0], ci), :])
             for ci in range(nch) for d in range(2)}

    def prepare(ci):
        for d, (rev, q_ref, k_ref, _, _, _, _, _, _) in enumerate(dirs):
            rows = rows_of(rev, ci)
            b_all = decay.pop((ci, d))
            last = 0 if rev else c - 1
            for h in range(nh):
                u = (ci, d, h)
                cols = pl.ds(h * HEAD_DIM, HEAD_DIM)
                b = b_all[:, h * HEAD_DIM:(h + 1) * HEAD_DIM]
                qf = q_ref[rows, cols].astype(F32)
                kf = k_ref[rows, cols].astype(F32)
                q1s[u] = (qf * jnp.exp(b)).astype(BF16)
                ebls[u] = jnp.exp(b[last:last + 1, :])
                if direct:
                    k2s[u] = (kf * jnp.exp(b[last:last + 1, :] - b)).astype(BF16)
                    scores[u] = _direct_scores(qf, kf, b)
                else:
                    k1f = kf * jnp.exp(-b)
                    k2s[u] = (k1f * ebls[u]).astype(BF16)
                    scores[u] = lax.dot_general(q1s[u], k1f.astype(BF16), _NT, preferred_element_type=F32)

    state = {(d, h): dirs[d][6][h] for d in range(2) for h in range(nh)}

    def advance(ci):
        for d, (rev, _, _, _, v_ref, o_ref, _, _, mask) in enumerate(dirs):
            rows = rows_of(rev, ci)
            for h in range(nh):
                u = (ci, d, h)
                cols = pl.ds(h * HEAD_DIM, HEAD_DIM)
                v = v_ref[rows, cols]
                a = jnp.where(mask, scores.pop(u), 0.0).astype(BF16)
                st = state[d, h]
                o = jnp.dot(jnp.concatenate([q1s.pop(u), a], axis=1),
                            jnp.concatenate([st.astype(BF16), v], axis=0), preferred_element_type=F32)
                o_ref[rows, cols] = o.astype(BF16)
                decay_col = jnp.broadcast_to(ebls.pop(u), (HEAD_DIM, HEAD_DIM)).T
                state[d, h] = st * decay_col + lax.dot_general(k2s.pop(u), v, _TN, preferred_element_type=F32)

    prepare(0)
    for ci in range(nch):
        if ci + 1 < nch:
            prepare(ci + 1)
        advance(ci)
    for (d, h), st in state.items():
        dirs[d][6][h] = st

    @pl.when(i == pl.num_programs(1) - 1)
    def _():
        sff_ref[...] = stf[...]
        sfb_ref[...] = stb[...]


def _hgrn_call(q, kf, kb, lff, lfb, v, s0f, s0b, chunk, direct, tb):
    bn, ln, hw = q.shape
    nh = hw // HEAD_DIM
    assert ln % tb == 0 and tb % chunk == 0
    n = ln // tb
    fwd = pl.BlockSpec((None, tb, hw), lambda b, i: (b, i, 0))
    bwd = pl.BlockSpec((None, tb, hw), lambda b, i: (b, n - 1 - i, 0))
    st = pl.BlockSpec((None, nh, HEAD_DIM, HEAD_DIM), lambda b, i: (b, 0, 0, 0))
    st_shape = jax.ShapeDtypeStruct((bn, nh, HEAD_DIM, HEAD_DIM), F32)
    o_shape = jax.ShapeDtypeStruct((bn, ln, hw), BF16)
    return pl.pallas_call(
        functools.partial(_hgrn_kernel, nh, chunk, direct),
        grid=(bn, n),
        in_specs=[fwd, fwd, fwd, fwd, bwd, bwd, bwd, bwd, st, st],
        out_specs=[fwd, bwd, st, st],
        out_shape=[o_shape, o_shape, st_shape, st_shape],
        scratch_shapes=[pltpu.VMEM((nh, HEAD_DIM, HEAD_DIM), F32)] * 2,
        compiler_params=_params(("parallel", "arbitrary")),
        name="hgrn_scan",
    )(q, kf, lff, v, q, kb, lfb, v, s0f, s0b)


def _mixer_tail(mix, w_ref, x_ref, g1_ref, ng1_ref, x1_ref):
    tm = mix.shape[0]
    rb = min(tm, TAIL_ROWS)
    gain1 = g1_ref[...] * ng1_ref[...]
    for r in range(tm // rb):
        rows = pl.ds(r * rb, rb)
        y = jnp.dot(mix[r * rb:(r + 1) * rb], w_ref[...], preferred_element_type=F32)
        x1_ref[rows, :] = x_ref[rows, :] + _rms(y, gain1)


def _mix0_kernel(seq_len, u_ref, up_ref, un_ref, of_ref, ob_ref, g_ref, pw_ref, ps_ref, og_ref, w_ref,
                 x_ref, g1_ref, ng1_ref, x1_ref, ubuf):
    i = pl.program_id(1)
    tm, pwid = u_ref.shape
    hal = POOL_HALO
    pg = pwid // len(POOL_WINDOWS)

    first = i == 0
    last = i == pl.num_programs(1) - 1
    ubuf[0:hal, :] = jnp.where(first, 0.0, up_ref[...].astype(F32))
    ubuf[hal:hal + tm, :] = u_ref[...].astype(F32)
    ubuf[hal + tm:, :] = jnp.where(last, 0.0, un_ref[...].astype(F32))

    edge = max(POOL_WINDOWS) // 2
    t_top = i * tm + lax.broadcasted_iota(jnp.int32, (edge, pg), 0)
    t_bot = t_top + (tm - edge)

    def clipped_mean(acc_rows, t, w):
        lo = jnp.maximum(t - w // 2, 0)
        hi = jnp.minimum(t - w // 2 + w, seq_len)
        return acc_rows / (hi - lo).astype(F32)

    parts = []
    for gi, w in enumerate(POOL_WINDOWS):
        cols = pl.ds(gi * pg, pg)
        acc = ubuf[pl.ds(hal - w // 2, tm), cols]
        for jj in range(1, w):
            acc = acc + ubuf[pl.ds(hal - w // 2 + jj, tm), cols]
        mean = jnp.concatenate([clipped_mean(acc[:edge], t_top, w), acc[edge:tm - edge] * (1.0 / w),
                                clipped_mean(acc[tm - edge:], t_bot, w)], axis=0)
        y = mean - ubuf[pl.ds(hal, tm), cols]
        y = jnp.dot(y.astype(BF16), pw_ref[gi], preferred_element_type=F32)
        parts.append((y * ps_ref[:, gi * pg:(gi + 1) * pg]).astype(BF16))

    osum = of_ref[...].astype(F32) + ob_ref[...].astype(F32)
    gate = g_ref[...].astype(F32)
    for h in range(osum.shape[1] // HEAD_DIM):
        sl = slice(h * HEAD_DIM, (h + 1) * HEAD_DIM)
        parts.append((_rms(osum[:, sl], og_ref[...]) * gate[:, sl]).astype(BF16))

    _mixer_tail(jnp.concatenate(parts, axis=-1), w_ref, x_ref, g1_ref, ng1_ref, x1_ref)


def _mix0_call(u, of, ob, g, pool_w, pool_scale, og, w_out, x, g1, ng1):
    bn, ln, d = x.shape
    pwid, hw = u.shape[2], g.shape[2]
    tm = _row_tile(ln)
    hb = tm // POOL_HALO
    nhb = ln // POOL_HALO
    prev = pl.BlockSpec((None, POOL_HALO, pwid), lambda b, i: (b, jnp.maximum(i * hb - 1, 0), 0))
    nxt = pl.BlockSpec((None, POOL_HALO, pwid), lambda b, i: (b, jnp.minimum((i + 1) * hb, nhb - 1), 0))
    return pl.pallas_call(
        functools.partial(_mix0_kernel, ln),
        grid=(bn, ln // tm),
        in_specs=[_rows(tm, pwid), prev, nxt, _rows(tm, hw), _rows(tm, hw), _rows(tm, hw),
                  _full(pool_w.shape), _full((1, pwid)), _full((1, HEAD_DIM)), _full(w_out.shape),
                  _rows(tm, d), _perb(d), _full((1, d))],
        out_specs=_rows(tm, d),
        out_shape=jax.ShapeDtypeStruct((bn, ln, d), F32),
        scratch_shapes=[pltpu.VMEM((tm + 2 * POOL_HALO, pwid), F32)],
        compiler_params=_params(("parallel", "parallel")),
        name="mix0",
    )(u, u, u, of, ob, g, pool_w, pool_scale, og, w_out, x, g1, ng1)


def _ffn_kernel(hc, x1_ref, ng2_ref, sc2_ref, sh2_ref, wi_ref, wo_ref, g2_ref, ng3_ref, o_ref, t_ref):
    hid = wo_ref.shape[0]
    tm = x1_ref.shape[0]
    rb = min(tm, TAIL_ROWS)
    gain2 = ng2_ref[...] * (1.0 + sc2_ref[...])
    for r in range(tm // rb):
        rows = pl.ds(r * rb, rb)
        h2 = (_rms(x1_ref[rows, :], gain2) + sh2_ref[...]).astype(BF16)
        for j in range(hid // hc):
            a = jnp.dot(h2, wi_ref[:, j * hc:(j + 1) * hc], preferred_element_type=F32)
            b = jnp.dot(h2, wi_ref[:, hid + j * hc:hid + (j + 1) * hc], preferred_element_type=F32)
            t_ref[rows, j * hc:(j + 1) * hc] = (_silu(a) * b).astype(BF16)
    gain3 = g2_ref[...] * ng3_ref[...]
    for r in range(tm // rb):
        rows = pl.ds(r * rb, rb)
        f = jnp.dot(t_ref[rows, :], wo_ref[...], preferred_element_type=F32)
        o_ref[rows, :] = x1_ref[rows, :] + _rms(f, gain3)


def _layer_block(shape, layer):
    nd = len(shape)
    return pl.BlockSpec((None,) + tuple(shape[1:]), lambda b, i: (layer,) + (0,) * (nd - 1),
                        pipeline_mode=pl.Buffered(1))


def _ffn_call(x1, ng2, sc2, sh2, w_in, w_out, layer, g2, ng3):
    bn, ln, d = x1.shape
    hid = w_out.shape[1]
    tm = _row_tile(ln)
    hc = 256
    assert hid % hc == 0
    return pl.pallas_call(
        functools.partial(_ffn_kernel, hc),
        grid=(bn, ln // tm),
        in_specs=[_rows(tm, d), _full((1, d)), _perb(d), _perb(d), _layer_block(w_in.shape, layer),
                  _layer_block(w_out.shape, layer), _perb(d), _full((1, d))],
        out_specs=_rows(tm, d),
        out_shape=jax.ShapeDtypeStruct((bn, ln, d), F32),
        scratch_shapes=[pltpu.VMEM((tm, hid), BF16)],
        compiler_params=_params(("parallel", "parallel")),
        name="ffn",
    )(x1, ng2, sc2, sh2, w_in, w_out, g2, ng3)


def _head_perm():
    n_freq = HEAD_DIM // 4
    blocks = np.arange(HEAD_DIM).reshape(2, 2, n_freq)
    return blocks.transpose(1, 0, 2).reshape(-1)


def _rope_tables(ln):
    half = HEAD_DIM // 2
    n_freq = HEAD_DIM // 4
    pos = np.arange(ln)
    inv = (ROPE_THETA ** (-np.arange(n_freq, dtype=np.float32) / n_freq)).astype(np.float32)
    lane = _head_perm()
    p = np.where(lane[None, :] < half, (pos // GRID_W)[:, None], (pos % GRID_W)[:, None]).astype(np.float32)
    ang = (p * inv[lane % n_freq][None, :]).astype(np.float32).astype(np.float64)
    sign = np.where(np.arange(HEAD_DIM) < half, -1.0, 1.0)
    return jnp.asarray(np.cos(ang), F32), jnp.asarray(np.sin(ang) * sign[None, :], F32)


def _gain_tables(g, rope):
    if rope is None:
        return jnp.concatenate([g, g], axis=1), None
    cos, sin = rope
    g_rolled = pltpu.roll(jnp.broadcast_to(g, (8, HEAD_DIM)), HEAD_DIM // 2, 1)[:1]
    gcos, gsin = cos * g, sin * g_rolled
    return jnp.concatenate([gcos, gcos], axis=1), jnp.concatenate([gsin, gsin], axis=1)


def _pair_matrices():
    n = 2 * HEAD_DIM
    r = lax.broadcasted_iota(jnp.int32, (n, n), 0)
    c = lax.broadcasted_iota(jnp.int32, (n, n), 1)
    same_head = (r // HEAD_DIM) == (c // HEAD_DIM)
    ones = jnp.where(same_head, 1.0, 0.0).astype(BF16)
    swap = jnp.where(same_head & ((r % HEAD_DIM) == ((c + HEAD_DIM // 2) % HEAD_DIM)), 1.0, 0.0).astype(BF16)
    return ones, swap


def _pair_stats(z, with_swap, ones, swap):
    ssq = jnp.dot((z * z).astype(BF16), ones, preferred_element_type=F32)
    swapped = jnp.dot(z.astype(BF16), swap, preferred_element_type=F32) if with_swap else None
    return ssq, swapped


def _pair_finish(z, stats, tables):
    gcos, gsin = tables
    ssq, swapped = stats
    y = z * gcos
    if swapped is not None:
        y = y + swapped * gsin
    return (y * lax.rsqrt(ssq * (1.0 / HEAD_DIM) + RMS_EPS)).astype(BF16)


def _proj1_kernel(with_q, with_rope, nq, nkv, *refs):
    x_ref, sc_ref, sh_ref, ng_ref, w_ref, qg_ref, kg_ref = refs[:7]
    refs = refs[7:]
    rope = None
    if with_rope:
        rope = (refs[0][...], refs[1][...])
        refs = refs[2:]
    h = (_rms(x_ref[...], ng_ref[...]) * (1.0 + sc_ref[...]) + sh_ref[...]).astype(BF16)
    dh = HEAD_DIM

    first = 0 if with_q else nq
    z = jnp.dot(h, w_ref[:, first * dh:], preferred_element_type=F32)

    def pair(hd):
        return z[:, (hd - first) * dh:(hd - first + 2) * dh]

    ones, swap = _pair_matrices()
    if with_q:
        q_ref, k_ref, v_ref = refs
        q_tables = _gain_tables(qg_ref[...] * QSCALE, rope)
    else:
        k_ref, v_ref = refs
    k_tables = _gain_tables(kg_ref[...], rope)
    todo = [(hq, q_ref, hq, q_tables) for hq in range(0, nq if with_q else 0, 2)]
    todo += [(nq + hk, k_ref, hk, k_tables) for hk in range(0, nkv, 2)]
    stats = [_pair_stats(pair(hd), with_rope, ones, swap) for hd, _, _, _ in todo]
    for (hd, o_ref, ho, tables), st in zip(todo, stats):
        o_ref[:, ho * dh:(ho + 2) * dh] = _pair_finish(pair(hd), st, tables)
    for hk in range(0, nkv, 2):
        v_ref[hk * dh:(hk + 2) * dh, :] = pair(nq + nkv + hk).T.astype(BF16)


def _proj1_call(x, sc, sh, ng, w, qg, kg, with_q, with_rope):
    bn, ln, d = x.shape
    dh = HEAD_DIM
    nkv = ATT_KV_HEADS
    nq = w.shape[1] // dh - 2 * nkv
    tm = _row_tile(ln)
    in_specs = [_rows(tm, d), _perb(d), _perb(d), _full((1, d)), _full(w.shape), _full((1, dh)), _full((1, dh))]
    args = [x, sc, sh, ng, w, qg, kg]
    if with_rope:
        tab = pl.BlockSpec((tm, dh), lambda b, i: (i, 0))
        in_specs += [tab, tab]
        args += list(_rope_tables(ln))
    out_specs = [_rows(tm, nkv * dh), pl.BlockSpec((None, nkv * dh, tm), lambda b, i: (b, 0, i))]
    out_shape = [jax.ShapeDtypeStruct((bn, ln, nkv * dh), BF16), jax.ShapeDtypeStruct((bn, nkv * dh, ln), BF16)]
    if with_q:
        out_specs = [_rows(tm, nq * dh)] + out_specs
        out_shape = [jax.ShapeDtypeStruct((bn, ln, nq * dh), BF16)] + out_shape
    return pl.pallas_call(
        functools.partial(_proj1_kernel, with_q, with_rope, nq, nkv),
        grid=(bn, ln // tm),
        in_specs=in_specs,
        out_specs=out_specs,
        out_shape=out_shape,
        compiler_params=_params(("parallel", "parallel")),
        name="proj1_q" if with_q else "proj1_kv",
    )(*args)


def _attn_kernel(group, tk, q_ref, k_ref, v_ref, o_ref, acc_ref, m_ref):
    dh = HEAD_DIM
    tq = q_ref.shape[0]
    nk = k_ref.shape[0] // tk
    acc_ref[...] = jnp.zeros_like(acc_ref)
    m_ref[...] = jnp.full_like(m_ref, -jnp.inf)
    ones = jnp.ones((tk, dh), BF16)

    def step(j, carry):
        rows = pl.ds(pl.multiple_of(j * tk, tk), tk)
        kc = k_ref[rows, :]
        vc = jnp.concatenate([v_ref[rows, :], ones], axis=-1)
        for g in range(group):
            qg = q_ref[:, g * dh:(g + 1) * dh]
            s = lax.dot_general(qg, kc, _NT, preferred_element_type=F32)
            m_old = m_ref[g]
            m_new = jnp.maximum(m_old, jnp.max(s, axis=1, keepdims=True))
            p = jnp.exp2(s - m_new).astype(BF16)
            alpha = jnp.exp2(m_old - m_new)
            acc_ref[g] = alpha * acc_ref[g] + jnp.dot(p, vc, preferred_element_type=F32)
            m_ref[g] = m_new
        return carry

    lax.fori_loop(0, nk, step, 0)
    for g in range(group):
        acc = acc_ref[g]
        o_ref[:, g * dh:(g + 1) * dh] = (acc[:, :dh] / acc[:, dh:dh + 1]).astype(BF16)


def _attn_call(q, k, v, tq, tk):
    bn, lq, dq = q.shape
    lk = k.shape[1]
    dh = HEAD_DIM
    nkv = k.shape[2] // dh
    group = dq // dh // nkv
    assert lq % tq == 0 and lk % tk == 0
    qspec = pl.BlockSpec((None, tq, group * dh), lambda b, h, i: (b, i, h))
    kvspec = pl.BlockSpec((None, lk, dh), lambda b, h, i: (b, 0, h))
    return pl.pallas_call(
        functools.partial(_attn_kernel, group, tk),
        grid=(bn, nkv, lq // tq),
        in_specs=[qspec, kvspec, kvspec],
        out_specs=qspec,
        out_shape=jax.ShapeDtypeStruct((bn, lq, dq), BF16),
        scratch_shapes=[pltpu.VMEM((group, tq, 2 * dh), F32), pltpu.VMEM((group, tq, 1), F32)],
        compiler_params=_params(("parallel", "parallel", "parallel")),
        name="gqa_attn",
    )(q, k, v)


def _attn_fast_kernel(group, bound_ref, q_ref, k_ref, vt_ref, o_ref, qt_ref):
    dh = HEAD_DIM
    nsub = k_ref.shape[0] // ATT_SUB
    ntile = q_ref.shape[0] // ATT_TQ
    for t in range(ntile):
        for g in range(group):
            qt_ref[t * group + g] = (q_ref[t * ATT_TQ:(t + 1) * ATT_TQ, g * dh:(g + 1) * dh]
                                     .astype(F32).T.astype(BF16))
    ones = jnp.ones((ONES_ROWS, ATT_SUB), BF16)
    bound = bound_ref[0]
    work = [(t, s, g) for t in range(ntile) for s in range(nsub) for g in range(group)]

    def scores(t, s, g):
        kc = k_ref[s * ATT_SUB:(s + 1) * ATT_SUB, :]
        return jnp.dot(kc, qt_ref[t * group + g], preferred_element_type=F32)

    pending = [scores(*w) for w in work[:ATT_AHEAD]]
    tot = {}
    for idx, (t, s, g) in enumerate(work):
        if idx + ATT_AHEAD < len(work):
            pending.append(scores(*work[idx + ATT_AHEAD]))
        pt = jnp.exp2(pending.pop(0) - bound).astype(BF16)
        vext = jnp.concatenate([vt_ref[:, s * ATT_SUB:(s + 1) * ATT_SUB], ones], axis=0)
        pv = jnp.dot(vext, pt, preferred_element_type=F32)
        tot[t, g] = pv if s == 0 else tot[t, g] + pv
        if s == nsub - 1:
            acc = tot.pop((t, g))
            o_ref[t * ATT_TQ:(t + 1) * ATT_TQ, g * dh:(g + 1) * dh] = (acc[:dh] / acc[dh:dh + 1]).T.astype(BF16)


def _attn_fast_call(bound, q, k, vt, tq):
    bn, lq, dq = q.shape
    lk = k.shape[1]
    dh = HEAD_DIM
    nkv = k.shape[2] // dh
    group = dq // dh // nkv
    assert lq % tq == 0 and tq % ATT_TQ == 0 and lk % ATT_SUB == 0
    qspec = pl.BlockSpec((None, tq, group * dh), lambda b, h, i: (b, i, h))
    return pl.pallas_call(
        functools.partial(_attn_fast_kernel, group),
        grid=(bn, nkv, lq // tq),
        in_specs=[pl.BlockSpec(memory_space=pltpu.SMEM), qspec,
                  pl.BlockSpec((None, lk, dh), lambda b, h, i: (b, 0, h)),
                  pl.BlockSpec((None, dh, lk), lambda b, h, i: (b, h, 0))],
        out_specs=qspec,
        out_shape=jax.ShapeDtypeStruct((bn, lq, dq), BF16),
        scratch_shapes=[pltpu.VMEM((tq // ATT_TQ * group, dh, ATT_TQ), BF16)],
        compiler_params=_params(("parallel", "parallel", "parallel")),
        name="gqa_attn_fast",
    )(bound, q, k, vt)


def _score_bound(qg, kg):
    return (1.02 * HEAD_DIM * QSCALE) * jnp.max(jnp.abs(qg)) * jnp.max(jnp.abs(kg))


def _attention(q, k, vt, qg, kg):
    bound = _score_bound(qg, kg).reshape(1)
    return lax.cond(bound[0] <= SAFE_BOUND,
                    lambda: _attn_fast_call(bound, q, k, vt, ATT_FAST_ROWS if q.shape[1] % ATT_FAST_ROWS == 0
                                            else ATT_TQ),
                    lambda: _attn_call(q, k, jnp.swapaxes(vt, 1, 2), 256, _kv_chunk(k.shape[1])))


def _kv_chunk(lk):
    for tk in (768, 512, 384, 256, 128):
        if lk % tk == 0:
            return tk
    raise ValueError(f"key length {lk} is not a multiple of 128")


def _mix1_kernel(o_ref, w_ref, x_ref, g1_ref, ng1_ref, x1_ref):
    _mixer_tail(o_ref[...], w_ref, x_ref, g1_ref, ng1_ref, x1_ref)


def _mix1_call(o, w_out, x, g1, ng1):
    bn, ln, d = x.shape
    tm = _row_tile(ln)
    return pl.pallas_call(
        _mix1_kernel,
        grid=(bn, ln // tm),
        in_specs=[_rows(tm, o.shape[2]), _full(w_out.shape), _rows(tm, d), _perb(d), _full((1, d))],
        out_specs=_rows(tm, d),
        out_shape=jax.ShapeDtypeStruct((bn, ln, d), F32),
        compiler_params=_params(("parallel", "parallel")),
        name="mix1",
    )(o, w_out, x, g1, ng1)


def kernel(x, c, ctx, c_ctx, ada_w, ada_b, norm_g, ab_w_in, ab_w_out, pool_w, pool_scale, hg_lower, hg_onorm_g,
           att_w_in, att_w_out, att_qnorm_g, att_knorm_g, ffn_w_in, ffn_w_out):
    bn, ln, d = x.shape
    depth = ada_w.shape[0]
    hw = hg_lower.shape[2]
    pwid = pool_scale.shape[1]
    nh = hw // HEAD_DIM

    pad = (-(bn + 1)) % 8
    cc = jnp.concatenate([c, c_ctx[None, :], jnp.zeros((pad, d), F32)], axis=0)
    mods = _ada_call(cc, ada_w, ada_b)

    def split_mods(l, ctx_rows):
        m = mods[l, bn:bn + 1] if ctx_rows else mods[l, :bn]
        m = jnp.broadcast_to(m, (bn, 6 * d)).reshape(bn, 1, 6, d)
        return [m[:, :, t, :] for t in range(6)]

    w_ffn_in = ffn_w_in.astype(BF16)
    w_ffn_out = ffn_w_out.astype(BF16)

    lat, cx = x, ctx
    for l in range(depth):
        j = l // 2
        need_ctx = l < depth - 1
        ng = [norm_g[l, t][None, :] for t in range(4)]
        m_lat = split_mods(l, False)
        m_ctx = split_mods(l, True)

        if l % 2 == 0:
            w_in = ab_w_in[j].astype(BF16)
            w_out = ab_w_out[j].astype(BF16)
            pw = pool_w[j].astype(BF16)
            ps = pool_scale[j][None, :]
            og = hg_onorm_g[j][None, :]
            hgl = hg_lower.reshape(hg_lower.shape[0], 2 * hw)
            pc = _proj0_call(cx, m_ctx[1], m_ctx[0], ng[0], w_in, hgl, j, pwid, hw)
            plat = _proj0_call(lat, m_lat[1], m_lat[0], ng[0], w_in, hgl, j, pwid, hw)
            zero = jnp.zeros((bn, nh, HEAD_DIM, HEAD_DIM), F32)
            u_c, q_c, kf_c, kb_c, lff_c, lfb_c, v_c, g_c = pc
            u_l, q_l, kf_l, kb_l, lff_l, lfb_l, v_l, g_l = plat

            def scan(chunk, direct, tb):
                oc_f, oc_b, s_f, s_b = _hgrn_call(q_c, kf_c, kb_c, lff_c, lfb_c, v_c, zero, zero, chunk, direct, tb)
                ol_f, ol_b, _, _ = _hgrn_call(q_l, kf_l, kb_l, lff_l, lfb_l, v_l, s_f, s_b, chunk, direct, tb)
                return oc_f, oc_b, ol_f, ol_b

            lbound = jnp.cumsum(jax.nn.softmax(hg_lower.astype(F32), axis=0), axis=0)[j]
            worst = HG_CHUNK * jnp.max(-jnp.log(lbound))
            of_c, ob_c, of_l, ob_l = lax.cond(worst <= HG_SAFE_DECAY,
                                              lambda: scan(HG_CHUNK, False, 256),
                                              lambda: scan(HG_SAFE_CHUNK, True, 128))
            x1_l = _mix0_call(u_l, of_l, ob_l, g_l, pw, ps, og, w_out, lat, m_lat[2], ng[1])
            if need_ctx:
                x1_c = _mix0_call(u_c, of_c, ob_c, g_c, pw, ps, og, w_out, cx, m_ctx[2], ng[1])
        else:
            perm = _head_perm()
            n_qk = att_w_in.shape[2] // HEAD_DIM - ATT_KV_HEADS
            cols = np.concatenate([hd * HEAD_DIM + perm for hd in range(n_qk)]
                                  + [np.arange(n_qk * HEAD_DIM, att_w_in.shape[2])])
            w_in = att_w_in[j][:, cols].astype(BF16)
            w_out = att_w_out[j].astype(BF16)
            qg = att_qnorm_g[j][perm][None, :]
            kg = att_knorm_g[j][perm][None, :]
            outs_c = _proj1_call(cx, m_ctx[1], m_ctx[0], ng[0], w_in, qg, kg, need_ctx, False)
            q_l, k_l, vt_l = _proj1_call(lat, m_lat[1], m_lat[0], ng[0], w_in, qg, kg, True, True)
            k_c, vt_c = outs_c[-2], outs_c[-1]
            k_all = jnp.concatenate([k_l, k_c], axis=1)
            vt_all = jnp.concatenate([vt_l, vt_c], axis=2)
            o_l = _attention(q_l, k_all, vt_all, qg, kg)
            x1_l = _mix1_call(o_l, w_out, lat, m_lat[2], ng[1])
            if need_ctx:
                o_c = _attn_call(outs_c[0], k_c, jnp.swapaxes(vt_c, 1, 2), 256, _kv_chunk(k_c.shape[1]))
                x1_c = _mix1_call(o_c, w_out, cx, m_ctx[2], ng[1])

        lat = _ffn_call(x1_l, ng[2], m_lat[4], m_lat[3], w_ffn_in, w_ffn_out, l, m_lat[5], ng[3])
        if need_ctx:
            cx = _ffn_call(x1_c, ng[2], m_ctx[4], m_ctx[3], w_ffn_in, w_ffn_out, l, m_ctx[5], ng[3])
    return lat
```

```python
import functools

import numpy as np
import jax
import jax.numpy as jnp
from jax import lax
from jax.experimental import pallas as pl
from jax.experimental.pallas import tpu as pltpu

F32 = jnp.float32
BF16 = jnp.bfloat16

RMS_EPS = 1e-6
GRID_W = 64
POOL_WINDOWS = (2, 4, 8, 16)
HEAD_DIM = 128
ATT_KV_HEADS = 2
ROPE_THETA = 10000.0
HG_CHUNK = 64
HG_SAFE_CHUNK = 16
HG_SAFE_DECAY = 80.0
TAIL_ROWS = 256
POOL_HALO = 16
LOG2E = 1.4426950408889634
QSCALE = HEAD_DIM ** -0.5 * LOG2E
ONES_ROWS = 16
ATT_TQ = 256
ATT_FAST_ROWS = 1024
ATT_SUB = 256
ATT_AHEAD = 3
SAFE_BOUND = 60.0

VMEM_LIMIT = 56 * 1024 * 1024

_NT = (((1,), (1,)), ((), ()))
_TN = (((0,), (0,)), ((), ()))


def _params(sem):
    return pltpu.CompilerParams(dimension_semantics=sem, vmem_limit_bytes=VMEM_LIMIT)


def _rows(tm, w):
    return pl.BlockSpec((None, tm, w), lambda b, i: (b, i, 0))


def _perb(w):
    return pl.BlockSpec((None, 1, w), lambda b, i: (b, 0, 0))


def _full(shape):
    nd = len(shape)
    return pl.BlockSpec(shape, lambda b, i: (0,) * nd, pipeline_mode=pl.Buffered(1))


def _sigmoid(z):
    return 0.5 * jnp.tanh(0.5 * z) + 0.5


def _silu(z):
    return z * _sigmoid(z)


def _rms(x, g):
    return x * lax.rsqrt(jnp.mean(x * x, axis=-1, keepdims=True) + RMS_EPS) * g


def _row_tile(n):
    return 512 if n % 512 == 0 else 256


def _ada_kernel(c_ref, w_ref, b_ref, o_ref):
    o_ref[...] = jnp.dot(_silu(c_ref[...]), w_ref[...], preferred_element_type=F32) + b_ref[...]


def _ada_call(cc, ada_w, ada_b):
    depth, d, n6 = ada_w.shape
    rows = cc.shape[0]
    tn = n6 // 4
    return pl.pallas_call(
        _ada_kernel,
        grid=(depth, n6 // tn),
        in_specs=[
            pl.BlockSpec((rows, d), lambda l, j: (0, 0)),
            pl.BlockSpec((None, d, tn), lambda l, j: (l, 0, j)),
            pl.BlockSpec((None, 1, tn), lambda l, j: (l, 0, j)),
        ],
        out_specs=pl.BlockSpec((None, rows, tn), lambda l, j: (l, 0, j)),
        out_shape=jax.ShapeDtypeStruct((depth, rows, n6), F32),
        compiler_params=_params(("arbitrary", "arbitrary")),
        name="ada_mod",
    )(cc, ada_w, ada_b.reshape(depth, 1, n6))


def _proj0_kernel(layer_j, pw, hw, x_ref, sc_ref, sh_ref, ng_ref, w_ref, hgl_ref,
                  u_ref, q_ref, kf_ref, kb_ref, lff_ref, lfb_ref, v_ref, g_ref):
    hgl = hgl_ref[...]
    e = jnp.exp(hgl - jnp.max(hgl, axis=0, keepdims=True))
    sm = e / jnp.sum(e, axis=0, keepdims=True)
    lbound = jnp.sum(sm[:layer_j + 1], axis=0, keepdims=True)

    h = (_rms(x_ref[...], ng_ref[...]) * (1.0 + sc_ref[...]) + sh_ref[...]).astype(BF16)

    def col(lo, width):
        return jnp.dot(h, w_ref[:, lo:lo + width], preferred_element_type=F32)

    u_ref[...] = col(0, pw).astype(BF16)
    q_ref[...] = _silu(col(pw, hw)).astype(BF16)
    for d, (k_ref, l_ref) in enumerate(((kf_ref, lff_ref), (kb_ref, lfb_ref))):
        lb = lbound[:, d * hw:(d + 1) * hw]
        f = lb + (1.0 - lb) * _sigmoid(col(pw + (1 + d) * hw, hw))
        k_ref[...] = (1.0 - f).astype(BF16)
        l_ref[...] = jnp.log(f)
    v_ref[...] = col(pw + 3 * hw, hw).astype(BF16)
    g_ref[...] = _silu(col(pw + 4 * hw, hw)).astype(BF16)


def _proj0_call(x, sc, sh, ng, w, hgl, layer_j, pw, hw):
    bn, ln, d = x.shape
    tm = _row_tile(ln)
    bf = jax.ShapeDtypeStruct((bn, ln, hw), BF16)
    ff = jax.ShapeDtypeStruct((bn, ln, hw), F32)
    return pl.pallas_call(
        functools.partial(_proj0_kernel, layer_j, pw, hw),
        grid=(bn, ln // tm),
        in_specs=[_rows(tm, d), _perb(d), _perb(d), _full((1, d)), _full(w.shape), _full(hgl.shape)],
        out_specs=[_rows(tm, pw)] + [_rows(tm, hw)] * 7,
        out_shape=[jax.ShapeDtypeStruct((bn, ln, pw), BF16), bf, bf, bf, ff, ff, bf, bf],
        compiler_params=_params(("parallel", "parallel")),
        name="proj0",
    )(x, sc, sh, ng, w, hgl)


def _split3(x):
    hi = x.astype(BF16)
    r = x - hi.astype(F32)
    mid = r.astype(BF16)
    lo = (r - mid.astype(F32)).astype(BF16)
    return hi, mid, lo


def _direct_scores(q, k, b):
    c = q.shape[0]
    lane = lax.broadcasted_iota(jnp.int32, (c, c), 1)
    a = jnp.zeros((c, c), F32)
    for s in range(c):
        e = jnp.exp(jnp.minimum(b - b[s:s + 1, :], 0.0))
        col = jnp.sum(q * e * k[s:s + 1, :], axis=1, keepdims=True)
        a = jnp.where(lane == s, col, a)
    return a


def _hgrn_kernel(nh, c, direct, qf_ref, kf_ref, lf_ref, vf_ref, qb_ref, kb_ref, lb_ref, vb_ref, s0f_ref, s0b_ref,
                 of_ref, ob_ref, sff_ref, sfb_ref, stf, stb):
    i = pl.program_id(1)
    nch = qf_ref.shape[0] // c

    @pl.when(i == 0)
    def _():
        stf[...] = s0f_ref[...]
        stb[...] = s0b_ref[...]

    r = lax.broadcasted_iota(jnp.int32, (c, c), 0)
    s = lax.broadcasted_iota(jnp.int32, (c, c), 1)
    lower = s <= r
    upper = s >= r
    tri_f = jnp.where(lower, 1.0, 0.0).astype(BF16)
    tri_b = jnp.where(upper, 1.0, 0.0).astype(BF16)

    def cumul(tri, lf):
        return jnp.dot(jnp.concatenate([tri, tri, tri], axis=1), jnp.concatenate(_split3(lf), axis=0),
                       preferred_element_type=F32)

    dirs = ((False, qf_ref, kf_ref, lf_ref, vf_ref, of_ref, stf, tri_f, lower),
            (True, qb_ref, kb_ref, lb_ref, vb_ref, ob_ref, stb, tri_b, upper))

    def rows_of(rev, ci):
        return pl.ds((nch - 1 - ci if rev else ci) * c, c)

    q1s, k2s, ebls, scores = {}, {}, {}, {}

    decay = {(ci, d): cumul(dirs[d][7], dirs[d][3][rows_of(dirs[d][0], ci), :])
             for ci in range(nch) for d in range(2)}

    def prepare(ci):
        for d, (rev, q_ref, k_ref, _, _, _, _, _, _) in enumerate(dirs):
            rows = rows_of(rev, ci)
            b_all = decay.pop((ci, d))
            last = 0 if rev else c - 1
            for h in range(nh):
                u = (ci, d, h)
                cols = pl.ds(h * HEAD_DIM, HEAD_DIM)
                b = b_all[:, h * HEAD_DIM:(h + 1) * HEAD_DIM]
                qf = q_ref[rows, cols].astype(F32)
                kf = k_ref[rows, cols].astype(F32)
                q1s[u] = (qf * jnp.exp(b)).astype(BF16)
                ebls[u] = jnp.exp(b[last:last + 1, :])
                if direct:
                    k2s[u] = (kf * jnp.exp(b[last:last + 1, :] - b)).astype(BF16)
                    scores[u] = _direct_scores(qf, kf, b)
                else:
                    k1f = kf * jnp.exp(-b)
                    k2s[u] = (k1f * ebls[u]).astype(BF16)
                    scores[u] = lax.dot_general(q1s[u], k1f.astype(BF16), _NT, preferred_element_type=F32)

    state = {(d, h): dirs[d][6][h] for d in range(2) for h in range(nh)}

    def advance(ci):
        for d, (rev, _, _, _, v_ref, o_ref, _, _, mask) in enumerate(dirs):
            rows = rows_of(rev, ci)
            for h in range(nh):
                u = (ci, d, h)
                cols = pl.ds(h * HEAD_DIM, HEAD_DIM)
                v = v_ref[rows, cols]
                a = jnp.where(mask, scores.pop(u), 0.0).astype(BF16)
                st = state[d, h]
                o = jnp.dot(jnp.concatenate([q1s.pop(u), a], axis=1),
                            jnp.concatenate([st.astype(BF16), v], axis=0), preferred_element_type=F32)
                o_ref[rows, cols] = o.astype(BF16)
                decay_col = jnp.broadcast_to(ebls.pop(u), (HEAD_DIM, HEAD_DIM)).T
                state[d, h] = st * decay_col + lax.dot_general(k2s.pop(u), v, _TN, preferred_element_type=F32)

    prepare(0)
    for ci in range(nch):
        if ci + 1 < nch:
            prepare(ci + 1)
        advance(ci)
    for (d, h), st in state.items():
        dirs[d][6][h] = st

    @pl.when(i == pl.num_programs(1) - 1)
    def _():
        sff_ref[...] = stf[...]
        sfb_ref[...] = stb[...]


def _hgrn_call(q, kf, kb, lff, lfb, v, s0f, s0b, chunk, direct, tb):
    bn, ln, hw = q.shape
    nh = hw // HEAD_DIM
    assert ln % tb == 0 and tb % chunk == 0
    n = ln // tb
    fwd = pl.BlockSpec((None, tb, hw), lambda b, i: (b, i, 0))
    bwd = pl.BlockSpec((None, tb, hw), lambda b, i: (b, n - 1 - i, 0))
    st = pl.BlockSpec((None, nh, HEAD_DIM, HEAD_DIM), lambda b, i: (b, 0, 0, 0))
    st_shape = jax.ShapeDtypeStruct((bn, nh, HEAD_DIM, HEAD_DIM), F32)
    o_shape = jax.ShapeDtypeStruct((bn, ln, hw), BF16)
    return pl.pallas_call(
        functools.partial(_hgrn_kernel, nh, chunk, direct),
        grid=(bn, n),
        in_specs=[fwd, fwd, fwd, fwd, bwd, bwd, bwd, bwd, st, st],
        out_specs=[fwd, bwd, st, st],
        out_shape=[o_shape, o_shape, st_shape, st_shape],
        scratch_shapes=[pltpu.VMEM((nh, HEAD_DIM, HEAD_DIM), F32)] * 2,
        compiler_params=_params(("parallel", "arbitrary")),
        name="hgrn_scan",
    )(q, kf, lff, v, q, kb, lfb, v, s0f, s0b)


def _mixer_tail(mix, w_ref, x_ref, g1_ref, ng1_ref, x1_ref):
    tm = mix.shape[0]
    rb = min(tm, TAIL_ROWS)
    gain1 = g1_ref[...] * ng1_ref[...]
    for r in range(tm // rb):
        rows = pl.ds(r * rb, rb)
        y = jnp.dot(mix[r * rb:(r + 1) * rb], w_ref[...], preferred_element_type=F32)
        x1_ref[rows, :] = x_ref[rows, :] + _rms(y, gain1)


def _mix0_kernel(seq_len, u_ref, up_ref, un_ref, of_ref, ob_ref, g_ref, pw_ref, ps_ref, og_ref, w_ref,
                 x_ref, g1_ref, ng1_ref, x1_ref, ubuf):
    i = pl.program_id(1)
    tm, pwid = u_ref.shape
    hal = POOL_HALO
    pg = pwid // len(POOL_WINDOWS)

    first = i == 0
    last = i == pl.num_programs(1) - 1
    ubuf[0:hal, :] = jnp.where(first, 0.0, up_ref[...].astype(F32))
    ubuf[hal:hal + tm, :] = u_ref[...].astype(F32)
    ubuf[hal + tm:, :] = jnp.where(last, 0.0, un_ref[...].astype(F32))

    edge = max(POOL_WINDOWS) // 2
    t_top = i * tm + lax.broadcasted_iota(jnp.int32, (edge, pg), 0)
    t_bot = t_top + (tm - edge)

    def clipped_mean(acc_rows, t, w):
        lo = jnp.maximum(t - w // 2, 0)
        hi = jnp.minimum(t - w // 2 + w, seq_len)
        return acc_rows / (hi - lo).astype(F32)

    parts = []
    for gi, w in enumerate(POOL_WINDOWS):
        cols = pl.ds(gi * pg, pg)
        acc = ubuf[pl.ds(hal - w // 2, tm), cols]
        for jj in range(1, w):
            acc = acc + ubuf[pl.ds(hal - w // 2 + jj, tm), cols]
        mean = jnp.concatenate([clipped_mean(acc[:edge], t_top, w), acc[edge:tm - edge] * (1.0 / w),
                                clipped_mean(acc[tm - edge:], t_bot, w)], axis=0)
        y = mean - ubuf[pl.ds(hal, tm), cols]
        y = jnp.dot(y.astype(BF16), pw_ref[gi], preferred_element_type=F32)
        parts.append((y * ps_ref[:, gi * pg:(gi + 1) * pg]).astype(BF16))

    osum = of_ref[...].astype(F32) + ob_ref[...].astype(F32)
    gate = g_ref[...].astype(F32)
    for h in range(osum.shape[1] // HEAD_DIM):
        sl = slice(h * HEAD_DIM, (h + 1) * HEAD_DIM)
        parts.append((_rms(osum[:, sl], og_ref[...]) * gate[:, sl]).astype(BF16))

    _mixer_tail(jnp.concatenate(parts, axis=-1), w_ref, x_ref, g1_ref, ng1_ref, x1_ref)


def _ffn_kernel(hc, x1_ref, ng2_ref, sc2_ref, sh2_ref, wi_ref, wo_ref, g2_ref, ng3_ref, o_ref, t_ref):
    hid = wo_ref.shape[0]
    tm = x1_ref.shape[0]
    rb = min(tm, TAIL_ROWS)
    gain2 = ng2_ref[...] * (1.0 + sc2_ref[...])
    for r in range(tm // rb):
        rows = pl.ds(r * rb, rb)
        h2 = (_rms(x1_ref[rows, :], gain2) + sh2_ref[...]).astype(BF16)
        for j in range(hid // hc):
            a = jnp.dot(h2, wi_ref[:, j * hc:(j + 1) * hc], preferred_element_type=F32)
            b = jnp.dot(h2, wi_ref[:, hid + j * hc:hid + (j + 1) * hc], preferred_element_type=F32)
            t_ref[rows, j * hc:(j + 1) * hc] = (_silu(a) * b).astype(BF16)
    f = jnp.dot(t_ref[...], wo_ref[...], preferred_element_type=F32)
    o_ref[...] = x1_ref[...] + _rms(f, g2_ref[...] * ng3_ref[...])


def _layer_block(shape, layer):
    nd = len(shape)
    return pl.BlockSpec((None,) + tuple(shape[1:]), lambda b, i: (layer,) + (0,) * (nd - 1),
                        pipeline_mode=pl.Buffered(1))


FFN_COLS = 256


def _ffn_specs(d, w_in, w_out, layer):
    assert w_out.shape[1] % FFN_COLS == 0
    return [_full((1, d)), _perb(d), _perb(d), _layer_block(w_in.shape, layer),
            _layer_block(w_out.shape, layer), _perb(d), _full((1, d))]


def _mix0_ffn_kernel(seq_len, u_ref, up_ref, un_ref, of_ref, ob_ref, g_ref, pw_ref, ps_ref, og_ref, wm_ref,
                     x_ref, g1_ref, ng1_ref, ng2_ref, sc2_ref, sh2_ref, wi_ref, wo_ref, g2_ref, ng3_ref,
                     out_ref, ubuf, t_ref, x1_ref):
    _mix0_kernel(seq_len, u_ref, up_ref, un_ref, of_ref, ob_ref, g_ref, pw_ref, ps_ref, og_ref, wm_ref,
                 x_ref, g1_ref, ng1_ref, x1_ref, ubuf)
    _ffn_kernel(FFN_COLS, x1_ref, ng2_ref, sc2_ref, sh2_ref, wi_ref, wo_ref, g2_ref, ng3_ref, out_ref, t_ref)


def _mix0_ffn_call(u, of, ob, g, pool_w, pool_scale, og, w_mix, x, g1, ng1, ng2, sc2, sh2, w_in, w_out, layer,
                   g2, ng3):
    bn, ln, d = x.shape
    pwid, hw = u.shape[2], g.shape[2]
    tm = _row_tile(ln)
    hb = tm // POOL_HALO
    nhb = ln // POOL_HALO
    prev = pl.BlockSpec((None, POOL_HALO, pwid), lambda b, i: (b, jnp.maximum(i * hb - 1, 0), 0))
    nxt = pl.BlockSpec((None, POOL_HALO, pwid), lambda b, i: (b, jnp.minimum((i + 1) * hb, nhb - 1), 0))
    return pl.pallas_call(
        functools.partial(_mix0_ffn_kernel, ln),
        grid=(bn, ln // tm),
        in_specs=[_rows(tm, pwid), prev, nxt, _rows(tm, hw), _rows(tm, hw), _rows(tm, hw),
                  _full(pool_w.shape), _full((1, pwid)), _full((1, HEAD_DIM)), _full(w_mix.shape),
                  _rows(tm, d), _perb(d), _full((1, d))] + _ffn_specs(d, w_in, w_out, layer),
        out_specs=_rows(tm, d),
        out_shape=jax.ShapeDtypeStruct((bn, ln, d), F32),
        scratch_shapes=[pltpu.VMEM((tm + 2 * POOL_HALO, pwid), F32), pltpu.VMEM((tm, w_out.shape[1]), BF16),
                        pltpu.VMEM((tm, d), F32)],
        compiler_params=_params(("parallel", "parallel")),
        name="mix0_ffn",
    )(u, u, u, of, ob, g, pool_w, pool_scale, og, w_mix, x, g1, ng1, ng2, sc2, sh2, w_in, w_out, g2, ng3)


def _head_perm():
    n_freq = HEAD_DIM // 4
    blocks = np.arange(HEAD_DIM).reshape(2, 2, n_freq)
    return blocks.transpose(1, 0, 2).reshape(-1)


def _rope_tables(ln):
    half = HEAD_DIM // 2
    n_freq = HEAD_DIM // 4
    pos = np.arange(ln)
    inv = (ROPE_THETA ** (-np.arange(n_freq, dtype=np.float32) / n_freq)).astype(np.float32)
    lane = _head_perm()
    p = np.where(lane[None, :] < half, (pos // GRID_W)[:, None], (pos % GRID_W)[:, None]).astype(np.float32)
    ang = (p * inv[lane % n_freq][None, :]).astype(np.float32).astype(np.float64)
    sign = np.where(np.arange(HEAD_DIM) < half, -1.0, 1.0)
    return jnp.asarray(np.cos(ang), F32), jnp.asarray(np.sin(ang) * sign[None, :], F32)


def _gain_tables(g, rope):
    if rope is None:
        return jnp.concatenate([g, g], axis=1), None
    cos, sin = rope
    g_rolled = pltpu.roll(jnp.broadcast_to(g, (8, HEAD_DIM)), HEAD_DIM // 2, 1)[:1]
    gcos, gsin = cos * g, sin * g_rolled
    return jnp.concatenate([gcos, gcos], axis=1), jnp.concatenate([gsin, gsin], axis=1)


def _pair_matrices():
    n = 2 * HEAD_DIM
    r = lax.broadcasted_iota(jnp.int32, (n, n), 0)
    c = lax.broadcasted_iota(jnp.int32, (n, n), 1)
    same_head = (r // HEAD_DIM) == (c // HEAD_DIM)
    ones = jnp.where(same_head, 1.0, 0.0).astype(BF16)
    swap = jnp.where(same_head & ((r % HEAD_DIM) == ((c + HEAD_DIM // 2) % HEAD_DIM)), 1.0, 0.0).astype(BF16)
    return ones, swap


def _pair_stats(z, with_swap, ones, swap):
    ssq = jnp.dot((z * z).astype(BF16), ones, preferred_element_type=F32)
    swapped = jnp.dot(z.astype(BF16), swap, preferred_element_type=F32) if with_swap else None
    return ssq, swapped


def _pair_finish(z, stats, tables):
    gcos, gsin = tables
    ssq, swapped = stats
    y = z * gcos
    if swapped is not None:
        y = y + swapped * gsin
    return (y * lax.rsqrt(ssq * (1.0 / HEAD_DIM) + RMS_EPS)).astype(BF16)


def _proj1_kernel(with_q, with_rope, nq, nkv, *refs):
    x_ref, sc_ref, sh_ref, ng_ref, w_ref, qg_ref, kg_ref = refs[:7]
    refs = refs[7:]
    rope = None
    if with_rope:
        rope = (refs[0][...], refs[1][...])
        refs = refs[2:]
    h = (_rms(x_ref[...], ng_ref[...]) * (1.0 + sc_ref[...]) + sh_ref[...]).astype(BF16)
    dh = HEAD_DIM

    first = 0 if with_q else nq
    z = jnp.dot(h, w_ref[:, first * dh:], preferred_element_type=F32)

    def pair(hd):
        return z[:, (hd - first) * dh:(hd - first + 2) * dh]

    ones, swap = _pair_matrices()
    if with_q:
        q_ref, k_ref, v_ref = refs
        q_tables = _gain_tables(qg_ref[...] * QSCALE, rope)
    else:
        k_ref, v_ref = refs
    k_tables = _gain_tables(kg_ref[...], rope)
    todo = [(hq, q_ref, hq, q_tables) for hq in range(0, nq if with_q else 0, 2)]
    todo += [(nq + hk, k_ref, hk, k_tables) for hk in range(0, nkv, 2)]
    stats = [_pair_stats(pair(hd), with_rope, ones, swap) for hd, _, _, _ in todo]
    for (hd, o_ref, ho, tables), st in zip(todo, stats):
        o_ref[:, ho * dh:(ho + 2) * dh] = _pair_finish(pair(hd), st, tables)
    for hk in range(0, nkv, 2):
        v_ref[hk * dh:(hk + 2) * dh, :] = pair(nq + nkv + hk).T.astype(BF16)


def _proj1_call(x, sc, sh, ng, w, qg, kg, with_q, with_rope):
    bn, ln, d = x.shape
    dh = HEAD_DIM
    nkv = ATT_KV_HEADS
    nq = w.shape[1] // dh - 2 * nkv
    tm = _row_tile(ln)
    in_specs = [_rows(tm, d), _perb(d), _perb(d), _full((1, d)), _full(w.shape), _full((1, dh)), _full((1, dh))]
    args = [x, sc, sh, ng, w, qg, kg]
    if with_rope:
        tab = pl.BlockSpec((tm, dh), lambda b, i: (i, 0))
        in_specs += [tab, tab]
        args += list(_rope_tables(ln))
    out_specs = [_rows(tm, nkv * dh), pl.BlockSpec((None, nkv * dh, tm), lambda b, i: (b, 0, i))]
    out_shape = [jax.ShapeDtypeStruct((bn, ln, nkv * dh), BF16), jax.ShapeDtypeStruct((bn, nkv * dh, ln), BF16)]
    if with_q:
        out_specs = [_rows(tm, nq * dh)] + out_specs
        out_shape = [jax.ShapeDtypeStruct((bn, ln, nq * dh), BF16)] + out_shape
    return pl.pallas_call(
        functools.partial(_proj1_kernel, with_q, with_rope, nq, nkv),
        grid=(bn, ln // tm),
        in_specs=in_specs,
        out_specs=out_specs,
        out_shape=out_shape,
        compiler_params=_params(("parallel", "parallel")),
        name="proj1_q" if with_q else "proj1_kv",
    )(*args)


def _attn_kernel(group, tk, q_ref, k_ref, v_ref, o_ref, acc_ref, m_ref):
    dh = HEAD_DIM
    tq = q_ref.shape[0]
    nk = k_ref.shape[0] // tk
    acc_ref[...] = jnp.zeros_like(acc_ref)
    m_ref[...] = jnp.full_like(m_ref, -jnp.inf)
    ones = jnp.ones((tk, dh), BF16)

    def step(j, carry):
        rows = pl.ds(pl.multiple_of(j * tk, tk), tk)
        kc = k_ref[rows, :]
        vc = jnp.concatenate([v_ref[rows, :], ones], axis=-1)
        for g in range(group):
            qg = q_ref[:, g * dh:(g + 1) * dh]
            s = lax.dot_general(qg, kc, _NT, preferred_element_type=F32)
            m_old = m_ref[g]
            m_new = jnp.maximum(m_old, jnp.max(s, axis=1, keepdims=True))
            p = jnp.exp2(s - m_new).astype(BF16)
            alpha = jnp.exp2(m_old - m_new)
            acc_ref[g] = alpha * acc_ref[g] + jnp.dot(p, vc, preferred_element_type=F32)
            m_ref[g] = m_new
        return carry

    lax.fori_loop(0, nk, step, 0)
    for g in range(group):
        acc = acc_ref[g]
        o_ref[:, g * dh:(g + 1) * dh] = (acc[:, :dh] / acc[:, dh:dh + 1]).astype(BF16)


def _attn_call(q, k, v, tq, tk):
    bn, lq, dq = q.shape
    lk = k.shape[1]
    dh = HEAD_DIM
    nkv = k.shape[2] // dh
    group = dq // dh // nkv
    assert lq % tq == 0 and lk % tk == 0
    qspec = pl.BlockSpec((None, tq, group * dh), lambda b, h, i: (b, i, h))
    kvspec = pl.BlockSpec((None, lk, dh), lambda b, h, i: (b, 0, h))
    return pl.pallas_call(
        functools.partial(_attn_kernel, group, tk),
        grid=(bn, nkv, lq // tq),
        in_specs=[qspec, kvspec, kvspec],
        out_specs=qspec,
        out_shape=jax.ShapeDtypeStruct((bn, lq, dq), BF16),
        scratch_shapes=[pltpu.VMEM((group, tq, 2 * dh), F32), pltpu.VMEM((group, tq, 1), F32)],
        compiler_params=_params(("parallel", "parallel", "parallel")),
        name="gqa_attn",
    )(q, k, v)


def _attn_fast_kernel(group, bound_ref, q_ref, k_ref, vt_ref, o_ref, qt_ref):
    dh = HEAD_DIM
    nsub = k_ref.shape[0] // ATT_SUB
    ntile = q_ref.shape[0] // ATT_TQ
    for t in range(ntile):
        for g in range(group):
            qt_ref[t * group + g] = (q_ref[t * ATT_TQ:(t + 1) * ATT_TQ, g * dh:(g + 1) * dh]
                                     .astype(F32).T.astype(BF16))
    ones = jnp.ones((ONES_ROWS, ATT_SUB), BF16)
    bound = bound_ref[0]
    work = [(t, s, g) for t in range(ntile) for s in range(nsub) for g in range(group)]

    def scores(t, s, g):
        kc = k_ref[s * ATT_SUB:(s + 1) * ATT_SUB, :]
        return jnp.dot(kc, qt_ref[t * group + g], preferred_element_type=F32)

    pending = [scores(*w) for w in work[:ATT_AHEAD]]
    tot = {}
    for idx, (t, s, g) in enumerate(work):
        if idx + ATT_AHEAD < len(work):
            pending.append(scores(*work[idx + ATT_AHEAD]))
        pt = jnp.exp2(pending.pop(0) - bound).astype(BF16)
        vext = jnp.concatenate([vt_ref[:, s * ATT_SUB:(s + 1) * ATT_SUB], ones], axis=0)
        pv = jnp.dot(vext, pt, preferred_element_type=F32)
        tot[t, g] = pv if s == 0 else tot[t, g] + pv
        if s == nsub - 1:
            acc = tot.pop((t, g))
            o_ref[t * ATT_TQ:(t + 1) * ATT_TQ, g * dh:(g + 1) * dh] = (acc[:dh] / acc[dh:dh + 1]).T.astype(BF16)


def _attn_fast_call(bound, q, k, vt, tq):
    bn, lq, dq = q.shape
    lk = k.shape[1]
    dh = HEAD_DIM
    nkv = k.shape[2] // dh
    group = dq // dh // nkv
    assert lq % tq == 0 and tq % ATT_TQ == 0 and lk % ATT_SUB == 0
    qspec = pl.BlockSpec((None, tq, group * dh), lambda b, h, i: (b, i, h))
    return pl.pallas_call(
        functools.partial(_attn_fast_kernel, group),
        grid=(bn, nkv, lq // tq),
        in_specs=[pl.BlockSpec(memory_space=pltpu.SMEM), qspec,
                  pl.BlockSpec((None, lk, dh), lambda b, h, i: (b, 0, h)),
                  pl.BlockSpec((None, dh, lk), lambda b, h, i: (b, h, 0))],
        out_specs=qspec,
        out_shape=jax.ShapeDtypeStruct((bn, lq, dq), BF16),
        scratch_shapes=[pltpu.VMEM((tq // ATT_TQ * group, dh, ATT_TQ), BF16)],
        compiler_params=_params(("parallel", "parallel", "parallel")),
        name="gqa_attn_fast",
    )(bound, q, k, vt)


def _score_bound(qg, kg):
    return (1.02 * HEAD_DIM * QSCALE) * jnp.max(jnp.abs(qg)) * jnp.max(jnp.abs(kg))


def _attention(q, k, vt, qg, kg):
    bound = _score_bound(qg, kg).reshape(1)
    return lax.cond(bound[0] <= SAFE_BOUND,
                    lambda: _attn_fast_call(bound, q, k, vt, ATT_FAST_ROWS if q.shape[1] % ATT_FAST_ROWS == 0
                                            else ATT_TQ),
                    lambda: _attn_call(q, k, jnp.swapaxes(vt, 1, 2), 256, _kv_chunk(k.shape[1])))


def _kv_chunk(lk):
    for tk in (768, 512, 384, 256, 128):
        if lk % tk == 0:
            return tk
    raise ValueError(f"key length {lk} is not a multiple of 128")


def _mix1_ffn_kernel(o_ref, wm_ref, x_ref, g1_ref, ng1_ref, ng2_ref, sc2_ref, sh2_ref, wi_ref, wo_ref,
                     g2_ref, ng3_ref, out_ref, t_ref, x1_ref):
    _mixer_tail(o_ref[...], wm_ref, x_ref, g1_ref, ng1_ref, x1_ref)
    _ffn_kernel(FFN_COLS, x1_ref, ng2_ref, sc2_ref, sh2_ref, wi_ref, wo_ref, g2_ref, ng3_ref, out_ref, t_ref)


def _mix1_ffn_call(o, w_mix, x, g1, ng1, ng2, sc2, sh2, w_in, w_out, layer, g2, ng3):
    bn, ln, d = x.shape
    tm = _row_tile(ln)
    return pl.pallas_call(
        _mix1_ffn_kernel,
        grid=(bn, ln // tm),
        in_specs=[_rows(tm, o.shape[2]), _full(w_mix.shape), _rows(tm, d), _perb(d), _full((1, d))]
        + _ffn_specs(d, w_in, w_out, layer),
        out_specs=_rows(tm, d),
        out_shape=jax.ShapeDtypeStruct((bn, ln, d), F32),
        scratch_shapes=[pltpu.VMEM((tm, w_out.shape[1]), BF16), pltpu.VMEM((tm, d), F32)],
        compiler_params=_params(("parallel", "parallel")),
        name="mix1_ffn",
    )(o, w_mix, x, g1, ng1, ng2, sc2, sh2, w_in, w_out, g2, ng3)


def kernel(x, c, ctx, c_ctx, ada_w, ada_b, norm_g, ab_w_in, ab_w_out, pool_w, pool_scale, hg_lower, hg_onorm_g,
           att_w_in, att_w_out, att_qnorm_g, att_knorm_g, ffn_w_in, ffn_w_out):
    bn, ln, d = x.shape
    depth = ada_w.shape[0]
    hw = hg_lower.shape[2]
    pwid = pool_scale.shape[1]
    nh = hw // HEAD_DIM

    pad = (-(bn + 1)) % 8
    cc = jnp.concatenate([c, c_ctx[None, :], jnp.zeros((pad, d), F32)], axis=0)
    mods = _ada_call(cc, ada_w, ada_b)

    def split_mods(l, ctx_rows):
        m = mods[l, bn:bn + 1] if ctx_rows else mods[l, :bn]
        m = jnp.broadcast_to(m, (bn, 6 * d)).reshape(bn, 1, 6, d)
        return [m[:, :, t, :] for t in range(6)]

    w_ffn_in = ffn_w_in.astype(BF16)
    w_ffn_out = ffn_w_out.astype(BF16)

    lat, cx = x, ctx
    for l in range(depth):
        j = l // 2
        need_ctx = l < depth - 1
        ng = [norm_g[l, t][None, :] for t in range(4)]
        m_lat = split_mods(l, False)
        m_ctx = split_mods(l, True)

        if l % 2 == 0:
            w_in = ab_w_in[j].astype(BF16)
            w_out = ab_w_out[j].astype(BF16)
            pw = pool_w[j].astype(BF16)
            ps = pool_scale[j][None, :]
            og = hg_onorm_g[j][None, :]
            hgl = hg_lower.reshape(hg_lower.shape[0], 2 * hw)
            pc = _proj0_call(cx, m_ctx[1], m_ctx[0], ng[0], w_in, hgl, j, pwid, hw)
            plat = _proj0_call(lat, m_lat[1], m_lat[0], ng[0], w_in, hgl, j, pwid, hw)
            zero = jnp.zeros((bn, nh, HEAD_DIM, HEAD_DIM), F32)
            u_c, q_c, kf_c, kb_c, lff_c, lfb_c, v_c, g_c = pc
            u_l, q_l, kf_l, kb_l, lff_l, lfb_l, v_l, g_l = plat

            def scan(chunk, direct, tb):
                oc_f, oc_b, s_f, s_b = _hgrn_call(q_c, kf_c, kb_c, lff_c, lfb_c, v_c, zero, zero, chunk, direct, tb)
                ol_f, ol_b, _, _ = _hgrn_call(q_l, kf_l, kb_l, lff_l, lfb_l, v_l, s_f, s_b, chunk, direct, tb)
                return oc_f, oc_b, ol_f, ol_b

            lbound = jnp.cumsum(jax.nn.softmax(hg_lower.astype(F32), axis=0), axis=0)[j]
            worst = HG_CHUNK * jnp.max(-jnp.log(lbound))
            of_c, ob_c, of_l, ob_l = lax.cond(worst <= HG_SAFE_DECAY,
                                              lambda: scan(HG_CHUNK, False, 256),
                                              lambda: scan(HG_SAFE_CHUNK, True, 128))
            lat = _mix0_ffn_call(u_l, of_l, ob_l, g_l, pw, ps, og, w_out, lat, m_lat[2], ng[1], ng[2],
                                 m_lat[4], m_lat[3], w_ffn_in, w_ffn_out, l, m_lat[5], ng[3])
            if need_ctx:
                cx = _mix0_ffn_call(u_c, of_c, ob_c, g_c, pw, ps, og, w_out, cx, m_ctx[2], ng[1], ng[2],
                                    m_ctx[4], m_ctx[3], w_ffn_in, w_ffn_out, l, m_ctx[5], ng[3])
        else:
            perm = _head_perm()
            n_qk = att_w_in.shape[2] // HEAD_DIM - ATT_KV_HEADS
            cols = np.concatenate([hd * HEAD_DIM + perm for hd in range(n_qk)]
                                  + [np.arange(n_qk * HEAD_DIM, att_w_in.shape[2])])
            w_in = att_w_in[j][:, cols].astype(BF16)
            w_out = att_w_out[j].astype(BF16)
            qg = att_qnorm_g[j][perm][None, :]
            kg = att_knorm_g[j][perm][None, :]
            outs_c = _proj1_call(cx, m_ctx[1], m_ctx[0], ng[0], w_in, qg, kg, need_ctx, False)
            q_l, k_l, vt_l = _proj1_call(lat, m_lat[1], m_lat[0], ng[0], w_in, qg, kg, True, True)
            k_c, vt_c = outs_c[-2], outs_c[-1]
            k_all = jnp.concatenate([k_l, k_c], axis=1)
            vt_all = jnp.concatenate([vt_l, vt_c], axis=2)
            o_l = _attention(q_l, k_all, vt_all, qg, kg)
            lat = _mix1_ffn_call(o_l, w_out, lat, m_lat[2], ng[1], ng[2], m_lat[4], m_lat[3],
                                 w_ffn_in, w_ffn_out, l, m_lat[5], ng[3])
            if need_ctx:
                o_c = _attn_call(outs_c[0], k_c, jnp.swapaxes(vt_c, 1, 2), 256, _kv_chunk(k_c.shape[1]))
                cx = _mix1_ffn_call(o_c, w_out, cx, m_ctx[2], ng[1], ng[2], m_ctx[4], m_ctx[3],
                                    w_ffn_in, w_ffn_out, l, m_ctx[5], ng[3])
    return lat
```

```python
import functools

import numpy as np
import jax
import jax.numpy as jnp
from jax import lax
from jax.experimental import pallas as pl
from jax.experimental.pallas import tpu as pltpu

F32 = jnp.float32
BF16 = jnp.bfloat16

RMS_EPS = 1e-6
GRID_W = 64
POOL_WINDOWS = (2, 4, 8, 16)
HEAD_DIM = 128
ATT_KV_HEADS = 2
ROPE_THETA = 10000.0
HG_CHUNK = 64
HG_SAFE_CHUNK = 16
HG_SAFE_DECAY = 80.0
TAIL_ROWS = 256
POOL_HALO = 16
LOG2E = 1.4426950408889634
QSCALE = HEAD_DIM ** -0.5 * LOG2E
ONES_ROWS = 16
ATT_TQ = 256
ATT_FAST_ROWS = 1024
ATT_SUB = 256
ATT_AHEAD = 3
SAFE_BOUND = 60.0

VMEM_LIMIT = 56 * 1024 * 1024

_NT = (((1,), (1,)), ((), ()))
_TN = (((0,), (0,)), ((), ()))


def _params(sem):
    return pltpu.CompilerParams(dimension_semantics=sem, vmem_limit_bytes=VMEM_LIMIT)


def _rows(tm, w):
    return pl.BlockSpec((None, tm, w), lambda b, i: (b, i, 0))


def _perb(w):
    return pl.BlockSpec((None, 1, w), lambda b, i: (b, 0, 0))


def _full(shape):
    nd = len(shape)
    return pl.BlockSpec(shape, lambda b, i: (0,) * nd, pipeline_mode=pl.Buffered(1))


def _sigmoid(z):
    return 0.5 * jnp.tanh(0.5 * z) + 0.5


def _silu(z):
    return z * _sigmoid(z)


def _rms(x, g):
    return x * lax.rsqrt(jnp.mean(x * x, axis=-1, keepdims=True) + RMS_EPS) * g


def _row_tile(n):
    return 512 if n % 512 == 0 else 256


def _ada_kernel(c_ref, w_ref, b_ref, o_ref):
    o_ref[...] = jnp.dot(_silu(c_ref[...]), w_ref[...], preferred_element_type=F32) + b_ref[...]


def _ada_call(cc, ada_w, ada_b):
    depth, d, n6 = ada_w.shape
    rows = cc.shape[0]
    tn = n6 // 4
    return pl.pallas_call(
        _ada_kernel,
        grid=(depth, n6 // tn),
        in_specs=[
            pl.BlockSpec((rows, d), lambda l, j: (0, 0)),
            pl.BlockSpec((None, d, tn), lambda l, j: (l, 0, j)),
            pl.BlockSpec((None, 1, tn), lambda l, j: (l, 0, j)),
        ],
        out_specs=pl.BlockSpec((None, rows, tn), lambda l, j: (l, 0, j)),
        out_shape=jax.ShapeDtypeStruct((depth, rows, n6), F32),
        compiler_params=_params(("arbitrary", "arbitrary")),
        name="ada_mod",
    )(cc, ada_w, ada_b.reshape(depth, 1, n6))


def _proj0_kernel(layer_j, pw, hw, x_ref, sc_ref, sh_ref, ng_ref, w_ref, hgl_ref,
                  u_ref, q_ref, kf_ref, kb_ref, lff_ref, lfb_ref, v_ref, g_ref):
    hgl = hgl_ref[...]
    e = jnp.exp(hgl - jnp.max(hgl, axis=0, keepdims=True))
    sm = e / jnp.sum(e, axis=0, keepdims=True)
    lbound = jnp.sum(sm[:layer_j + 1], axis=0, keepdims=True)

    h = (_rms(x_ref[...], ng_ref[...]) * (1.0 + sc_ref[...]) + sh_ref[...]).astype(BF16)

    def col(lo, width):
        return jnp.dot(h, w_ref[:, lo:lo + width], preferred_element_type=F32)

    u_ref[...] = col(0, pw).astype(BF16)
    q_ref[...] = _silu(col(pw, hw)).astype(BF16)
    for d, (k_ref, l_ref) in enumerate(((kf_ref, lff_ref), (kb_ref, lfb_ref))):
        lb = lbound[:, d * hw:(d + 1) * hw]
        f = lb + (1.0 - lb) * _sigmoid(col(pw + (1 + d) * hw, hw))
        k_ref[...] = (1.0 - f).astype(BF16)
        l_ref[...] = jnp.log(f)
    v_ref[...] = col(pw + 3 * hw, hw).astype(BF16)
    g_ref[...] = _silu(col(pw + 4 * hw, hw)).astype(BF16)


def _proj0_call(x, sc, sh, ng, w, hgl, layer_j, pw, hw):
    bn, ln, d = x.shape
    tm = _row_tile(ln)
    bf = jax.ShapeDtypeStruct((bn, ln, hw), BF16)
    ff = jax.ShapeDtypeStruct((bn, ln, hw), F32)
    return pl.pallas_call(
        functools.partial(_proj0_kernel, layer_j, pw, hw),
        grid=(bn, ln // tm),
        in_specs=[_rows(tm, d), _perb(d), _perb(d), _full((1, d)), _full(w.shape), _full(hgl.shape)],
        out_specs=[_rows(tm, pw)] + [_rows(tm, hw)] * 7,
        out_shape=[jax.ShapeDtypeStruct((bn, ln, pw), BF16), bf, bf, bf, ff, ff, bf, bf],
        compiler_params=_params(("parallel", "parallel")),
        name="proj0",
    )(x, sc, sh, ng, w, hgl)


def _split3(x):
    hi = x.astype(BF16)
    r = x - hi.astype(F32)
    mid = r.astype(BF16)
    lo = (r - mid.astype(F32)).astype(BF16)
    return hi, mid, lo


def _direct_scores(q, k, b):
    c = q.shape[0]
    lane = lax.broadcasted_iota(jnp.int32, (c, c), 1)
    a = jnp.zeros((c, c), F32)
    for s in range(c):
        e = jnp.exp(jnp.minimum(b - b[s:s + 1, :], 0.0))
        col = jnp.sum(q * e * k[s:s + 1, :], axis=1, keepdims=True)
        a = jnp.where(lane == s, col, a)
    return a


def _hgrn_kernel(nh, c, direct, qf_ref, kf_ref, lf_ref, vf_ref, qb_ref, kb_ref, lb_ref, vb_ref, s0f_ref, s0b_ref,
                 of_ref, ob_ref, sff_ref, sfb_ref, stf, stb):
    i = pl.program_id(1)
    nch = qf_ref.shape[0] // c

    @pl.when(i == 0)
    def _():
        stf[...] = s0f_ref[...]
        stb[...] = s0b_ref[...]

    r = lax.broadcasted_iota(jnp.int32, (c, c), 0)
    s = lax.broadcasted_iota(jnp.int32, (c, c), 1)
    lower = s <= r
    upper = s >= r
    tri_f = jnp.where(lower, 1.0, 0.0).astype(BF16)
    tri_b = jnp.where(upper, 1.0, 0.0).astype(BF16)

    def cumul(tri, lf):
        return jnp.dot(jnp.concatenate([tri, tri, tri], axis=1), jnp.concatenate(_split3(lf), axis=0),
                       preferred_element_type=F32)

    dirs = ((False, qf_ref, kf_ref, lf_ref, vf_ref, of_ref, stf, tri_f, lower),
            (True, qb_ref, kb_ref, lb_ref, vb_ref, ob_ref, stb, tri_b, upper))

    def rows_of(rev, ci):
        return pl.ds((nch - 1 - ci if rev else ci) * c, c)

    q1s, k2s, ebls, scores = {}, {}, {}, {}

    decay = {(ci, d): cumul(dirs[d][7], dirs[d][3][rows_of(dirs[d][0], ci), :])
             for ci in range(nch) for d in range(2)}

    def prepare(ci):
        for d, (rev, q_ref, k_ref, _, _, _, _, _, _) in enumerate(dirs):
            rows = rows_of(rev, ci)
            b_all = decay.pop((ci, d))
            last = 0 if rev else c - 1
            for h in range(nh):
                u = (ci, d, h)
                cols = pl.ds(h * HEAD_DIM, HEAD_DIM)
                b = b_all[:, h * HEAD_DIM:(h + 1) * HEAD_DIM]
                qf = q_ref[rows, cols].astype(F32)
                kf = k_ref[rows, cols].astype(F32)
                q1s[u] = (qf * jnp.exp(b)).astype(BF16)
                ebls[u] = jnp.exp(b[last:last + 1, :])
                if direct:
                    k2s[u] = (kf * jnp.exp(b[last:last + 1, :] - b)).astype(BF16)
                    scores[u] = _direct_scores(qf, kf, b)
                else:
                    k1f = kf * jnp.exp(-b)
                    k2s[u] = (k1f * ebls[u]).astype(BF16)
                    scores[u] = lax.dot_general(q1s[u], k1f.astype(BF16), _NT, preferred_element_type=F32)

    state = {(d, h): dirs[d][6][h] for d in range(2) for h in range(nh)}

    def advance(ci):
        for d, (rev, _, _, _, v_ref, o_ref, _, _, mask) in enumerate(dirs):
            rows = rows_of(rev, ci)
            for h in range(nh):
                u = (ci, d, h)
                cols = pl.ds(h * HEAD_DIM, HEAD_DIM)
                v = v_ref[rows, cols]
                a = jnp.where(mask, scores.pop(u), 0.0).astype(BF16)
                st = state[d, h]
                o = jnp.dot(jnp.concatenate([q1s.pop(u), a], axis=1),
                            jnp.concatenate([st.astype(BF16), v], axis=0), preferred_element_type=F32)
                o_ref[rows, cols] = o.astype(BF16)
                decay_col = jnp.broadcast_to(ebls.pop(u), (HEAD_DIM, HEAD_DIM)).T
                state[d, h] = st * decay_col + lax.dot_general(k2s.pop(u), v, _TN, preferred_element_type=F32)

    prepare(0)
    for ci in range(nch):
        if ci + 1 < nch:
            prepare(ci + 1)
        advance(ci)
    for (d, h), st in state.items():
        dirs[d][6][h] = st

    @pl.when(i == pl.num_programs(1) - 1)
    def _():
        sff_ref[...] = stf[...]
        sfb_ref[...] = stb[...]


def _hgrn_call(q, kf, kb, lff, lfb, v, s0f, s0b, chunk, direct, tb):
    bn, ln, hw = q.shape
    nh = hw // HEAD_DIM
    assert ln % tb == 0 and tb % chunk == 0
    n = ln // tb
    fwd = pl.BlockSpec((None, tb, hw), lambda b, i: (b, i, 0))
    bwd = pl.BlockSpec((None, tb, hw), lambda b, i: (b, n - 1 - i, 0))
    st = pl.BlockSpec((None, nh, HEAD_DIM, HEAD_DIM), lambda b, i: (b, 0, 0, 0))
    st_shape = jax.ShapeDtypeStruct((bn, nh, HEAD_DIM, HEAD_DIM), F32)
    o_shape = jax.ShapeDtypeStruct((bn, ln, hw), BF16)
    return pl.pallas_call(
        functools.partial(_hgrn_kernel, nh, chunk, direct),
        grid=(bn, n),
        in_specs=[fwd, fwd, fwd, fwd, bwd, bwd, bwd, bwd, st, st],
        out_specs=[fwd, bwd, st, st],
        out_shape=[o_shape, o_shape, st_shape, st_shape],
        scratch_shapes=[pltpu.VMEM((nh, HEAD_DIM, HEAD_DIM), F32)] * 2,
        compiler_params=_params(("parallel", "arbitrary")),
        name="hgrn_scan",
    )(q, kf, lff, v, q, kb, lfb, v, s0f, s0b)


def _mixer_tail(mix, w_ref, x_ref, g1_ref, ng1_ref, x1_ref):
    tm = mix.shape[0]
    rb = min(tm, TAIL_ROWS)
    gain1 = g1_ref[...] * ng1_ref[...]
    for r in range(tm // rb):
        rows = pl.ds(r * rb, rb)
        y = jnp.dot(mix[r * rb:(r + 1) * rb], w_ref[...], preferred_element_type=F32)
        x1_ref[rows, :] = x_ref[rows, :] + _rms(y, gain1)


def _mix0_kernel(seq_len, u_ref, up_ref, un_ref, of_ref, ob_ref, g_ref, pw_ref, ps_ref, og_ref, w_ref,
                 x_ref, g1_ref, ng1_ref, x1_ref, ubuf):
    i = pl.program_id(1)
    tm, pwid = u_ref.shape
    hal = POOL_HALO
    pg = pwid // len(POOL_WINDOWS)

    first = i == 0
    last = i == pl.num_programs(1) - 1
    ubuf[0:hal, :] = jnp.where(first, 0.0, up_ref[...].astype(F32))
    ubuf[hal:hal + tm, :] = u_ref[...].astype(F32)
    ubuf[hal + tm:, :] = jnp.where(last, 0.0, un_ref[...].astype(F32))

    edge = max(POOL_WINDOWS) // 2
    t_top = i * tm + lax.broadcasted_iota(jnp.int32, (edge, pg), 0)
    t_bot = t_top + (tm - edge)

    def clipped_mean(acc_rows, t, w):
        lo = jnp.maximum(t - w // 2, 0)
        hi = jnp.minimum(t - w // 2 + w, seq_len)
        return acc_rows / (hi - lo).astype(F32)

    parts = []
    for gi, w in enumerate(POOL_WINDOWS):
        cols = pl.ds(gi * pg, pg)
        acc = ubuf[pl.ds(hal - w // 2, tm), cols]
        for jj in range(1, w):
            acc = acc + ubuf[pl.ds(hal - w // 2 + jj, tm), cols]
        mean = jnp.concatenate([clipped_mean(acc[:edge], t_top, w), acc[edge:tm - edge] * (1.0 / w),
                                clipped_mean(acc[tm - edge:], t_bot, w)], axis=0)
        y = mean - ubuf[pl.ds(hal, tm), cols]
        y = jnp.dot(y.astype(BF16), pw_ref[gi], preferred_element_type=F32)
        parts.append((y * ps_ref[:, gi * pg:(gi + 1) * pg]).astype(BF16))

    osum = of_ref[...].astype(F32) + ob_ref[...].astype(F32)
    gate = g_ref[...].astype(F32)
    for h in range(osum.shape[1] // HEAD_DIM):
        sl = slice(h * HEAD_DIM, (h + 1) * HEAD_DIM)
        parts.append((_rms(osum[:, sl], og_ref[...]) * gate[:, sl]).astype(BF16))

    _mixer_tail(jnp.concatenate(parts, axis=-1), w_ref, x_ref, g1_ref, ng1_ref, x1_ref)


def _ffn_kernel(hc, x1_ref, ng2_ref, sc2_ref, sh2_ref, wi_ref, wo_ref, g2_ref, ng3_ref, o_ref, t_ref):
    hid = wo_ref.shape[0]
    tm = x1_ref.shape[0]
    rb = min(tm, TAIL_ROWS)
    gain2 = ng2_ref[...] * (1.0 + sc2_ref[...])
    for r in range(tm // rb):
        rows = pl.ds(r * rb, rb)
        h2 = (_rms(x1_ref[rows, :], gain2) + sh2_ref[...]).astype(BF16)
        for j in range(hid // hc):
            a = jnp.dot(h2, wi_ref[:, j * hc:(j + 1) * hc], preferred_element_type=F32)
            b = jnp.dot(h2, wi_ref[:, hid + j * hc:hid + (j + 1) * hc], preferred_element_type=F32)
            t_ref[rows, j * hc:(j + 1) * hc] = (_silu(a) * b).astype(BF16)
    f = jnp.dot(t_ref[...], wo_ref[...], preferred_element_type=F32)
    o_ref[...] = x1_ref[...] + _rms(f, g2_ref[...] * ng3_ref[...])


def _layer_block(shape, layer):
    nd = len(shape)
    return pl.BlockSpec((None,) + tuple(shape[1:]), lambda b, i: (layer,) + (0,) * (nd - 1),
                        pipeline_mode=pl.Buffered(1))


FFN_COLS = 256


def _ffn_specs(d, w_in, w_out, layer):
    assert w_out.shape[1] % FFN_COLS == 0
    return [_full((1, d)), _perb(d), _perb(d), _layer_block(w_in.shape, layer),
            _layer_block(w_out.shape, layer), _perb(d), _full((1, d))]


def _mix0_ffn_kernel(seq_len, u_ref, up_ref, un_ref, of_ref, ob_ref, g_ref, pw_ref, ps_ref, og_ref, wm_ref,
                     x_ref, g1_ref, ng1_ref, ng2_ref, sc2_ref, sh2_ref, wi_ref, wo_ref, g2_ref, ng3_ref,
                     out_ref, ubuf, t_ref, x1_ref):
    _mix0_kernel(seq_len, u_ref, up_ref, un_ref, of_ref, ob_ref, g_ref, pw_ref, ps_ref, og_ref, wm_ref,
                 x_ref, g1_ref, ng1_ref, x1_ref, ubuf)
    _ffn_kernel(FFN_COLS, x1_ref, ng2_ref, sc2_ref, sh2_ref, wi_ref, wo_ref, g2_ref, ng3_ref, out_ref, t_ref)


def _mix0_ffn_call(u, of, ob, g, pool_w, pool_scale, og, w_mix, x, g1, ng1, ng2, sc2, sh2, w_in, w_out, layer,
                   g2, ng3):
    bn, ln, d = x.shape
    pwid, hw = u.shape[2], g.shape[2]
    tm = _row_tile(ln)
    hb = tm // POOL_HALO
    nhb = ln // POOL_HALO
    prev = pl.BlockSpec((None, POOL_HALO, pwid), lambda b, i: (b, jnp.maximum(i * hb - 1, 0), 0))
    nxt = pl.BlockSpec((None, POOL_HALO, pwid), lambda b, i: (b, jnp.minimum((i + 1) * hb, nhb - 1), 0))
    return pl.pallas_call(
        functools.partial(_mix0_ffn_kernel, ln),
        grid=(bn, ln // tm),
        in_specs=[_rows(tm, pwid), prev, nxt, _rows(tm, hw), _rows(tm, hw), _rows(tm, hw),
                  _full(pool_w.shape), _full((1, pwid)), _full((1, HEAD_DIM)), _full(w_mix.shape),
                  _rows(tm, d), _perb(d), _full((1, d))] + _ffn_specs(d, w_in, w_out, layer),
        out_specs=_rows(tm, d),
        out_shape=jax.ShapeDtypeStruct((bn, ln, d), F32),
        scratch_shapes=[pltpu.VMEM((tm + 2 * POOL_HALO, pwid), F32), pltpu.VMEM((tm, w_out.shape[1]), BF16),
                        pltpu.VMEM((tm, d), F32)],
        compiler_params=_params(("parallel", "parallel")),
        name="mix0_ffn",
    )(u, u, u, of, ob, g, pool_w, pool_scale, og, w_mix, x, g1, ng1, ng2, sc2, sh2, w_in, w_out, g2, ng3)


def _head_perm():
    n_freq = HEAD_DIM // 4
    blocks = np.arange(HEAD_DIM).reshape(2, 2, n_freq)
    return blocks.transpose(1, 0, 2).reshape(-1)


def _rope_tables(ln):
    half = HEAD_DIM // 2
    n_freq = HEAD_DIM // 4
    pos = np.arange(ln)
    inv = (ROPE_THETA ** (-np.arange(n_freq, dtype=np.float32) / n_freq)).astype(np.float32)
    lane = _head_perm()
    p = np.where(lane[None, :] < half, (pos // GRID_W)[:, None], (pos % GRID_W)[:, None]).astype(np.float32)
    ang = (p * inv[lane % n_freq][None, :]).astype(np.float32).astype(np.float64)
    sign = np.where(np.arange(HEAD_DIM) < half, -1.0, 1.0)
    return jnp.asarray(np.cos(ang), F32), jnp.asarray(np.sin(ang) * sign[None, :], F32)


def _gain_tables(g, rope):
    if rope is None:
        return jnp.concatenate([g, g], axis=1), None
    cos, sin = rope
    g_rolled = pltpu.roll(jnp.broadcast_to(g, (8, HEAD_DIM)), HEAD_DIM // 2, 1)[:1]
    gcos, gsin = cos * g, sin * g_rolled
    return jnp.concatenate([gcos, gcos], axis=1), jnp.concatenate([gsin, gsin], axis=1)


def _pair_matrices():
    n = 2 * HEAD_DIM
    r = lax.broadcasted_iota(jnp.int32, (n, n), 0)
    c = lax.broadcasted_iota(jnp.int32, (n, n), 1)
    same_head = (r // HEAD_DIM) == (c // HEAD_DIM)
    ones = jnp.where(same_head, 1.0, 0.0).astype(BF16)
    swap = jnp.where(same_head & ((r % HEAD_DIM) == ((c + HEAD_DIM // 2) % HEAD_DIM)), 1.0, 0.0).astype(BF16)
    return ones, swap


def _pair_stats(z, with_swap, ones, swap):
    ssq = jnp.dot((z * z).astype(BF16), ones, preferred_element_type=F32)
    swapped = jnp.dot(z.astype(BF16), swap, preferred_element_type=F32) if with_swap else None
    return ssq, swapped


def _pair_finish(z, stats, tables):
    gcos, gsin = tables
    ssq, swapped = stats
    y = z * gcos
    if swapped is not None:
        y = y + swapped * gsin
    return (y * lax.rsqrt(ssq * (1.0 / HEAD_DIM) + RMS_EPS)).astype(BF16)


def _proj1_kernel(with_q, with_rope, nq, nkv, *refs):
    x_ref, sc_ref, sh_ref, ng_ref, w_ref, qg_ref, kg_ref = refs[:7]
    refs = refs[7:]
    rope = None
    if with_rope:
        rope = (refs[0][...], refs[1][...])
        refs = refs[2:]
    h = (_rms(x_ref[...], ng_ref[...]) * (1.0 + sc_ref[...]) + sh_ref[...]).astype(BF16)
    dh = HEAD_DIM

    first = 0 if with_q else nq
    z = jnp.dot(h, w_ref[:, first * dh:], preferred_element_type=F32)

    def pair(hd):
        return z[:, (hd - first) * dh:(hd - first + 2) * dh]

    ones, swap = _pair_matrices()
    if with_q:
        q_ref, k_ref, v_ref = refs
        q_tables = _gain_tables(qg_ref[...] * QSCALE, rope)
    else:
        k_ref, v_ref = refs
    k_tables = _gain_tables(kg_ref[...], rope)
    todo = [(hq, q_ref, hq, q_tables) for hq in range(0, nq if with_q else 0, 2)]
    todo += [(nq + hk, k_ref, hk, k_tables) for hk in range(0, nkv, 2)]
    stats = [_pair_stats(pair(hd), with_rope, ones, swap) for hd, _, _, _ in todo]
    for (hd, o_ref, ho, tables), st in zip(todo, stats):
        o_ref[:, ho * dh:(ho + 2) * dh] = _pair_finish(pair(hd), st, tables)
    for hk in range(0, nkv, 2):
        v_ref[hk * dh:(hk + 2) * dh, :] = pair(nq + nkv + hk).T.astype(BF16)


def _proj1_call(x, sc, sh, ng, w, qg, kg, with_q, with_rope):
    bn, ln, d = x.shape
    dh = HEAD_DIM
    nkv = ATT_KV_HEADS
    nq = w.shape[1] // dh - 2 * nkv
    tm = _row_tile(ln)
    in_specs = [_rows(tm, d), _perb(d), _perb(d), _full((1, d)), _full(w.shape), _full((1, dh)), _full((1, dh))]
    args = [x, sc, sh, ng, w, qg, kg]
    if with_rope:
        tab = pl.BlockSpec((tm, dh), lambda b, i: (i, 0))
        in_specs += [tab, tab]
        args += list(_rope_tables(ln))
    out_specs = [_rows(tm, nkv * dh), pl.BlockSpec((None, nkv * dh, tm), lambda b, i: (b, 0, i))]
    out_shape = [jax.ShapeDtypeStruct((bn, ln, nkv * dh), BF16), jax.ShapeDtypeStruct((bn, nkv * dh, ln), BF16)]
    if with_q:
        out_specs = [_rows(tm, nq * dh)] + out_specs
        out_shape = [jax.ShapeDtypeStruct((bn, ln, nq * dh), BF16)] + out_shape
    return pl.pallas_call(
        functools.partial(_proj1_kernel, with_q, with_rope, nq, nkv),
        grid=(bn, ln // tm),
        in_specs=in_specs,
        out_specs=out_specs,
        out_shape=out_shape,
        compiler_params=_params(("parallel", "parallel")),
        name="proj1_q" if with_q else "proj1_kv",
    )(*args)


def _attn_kernel(group, tk, q_ref, k_ref, v_ref, o_ref, acc_ref, m_ref):
    dh = HEAD_DIM
    tq = q_ref.shape[0]
    nk = k_ref.shape[0] // tk
    acc_ref[...] = jnp.zeros_like(acc_ref)
    m_ref[...] = jnp.full_like(m_ref, -jnp.inf)
    ones = jnp.ones((tk, dh), BF16)

    def step(j, carry):
        rows = pl.ds(pl.multiple_of(j * tk, tk), tk)
        kc = k_ref[rows, :]
        vc = jnp.concatenate([v_ref[rows, :], ones], axis=-1)
        for g in range(group):
            qg = q_ref[:, g * dh:(g + 1) * dh]
            s = lax.dot_general(qg, kc, _NT, preferred_element_type=F32)
            m_old = m_ref[g]
            m_new = jnp.maximum(m_old, jnp.max(s, axis=1, keepdims=True))
            p = jnp.exp2(s - m_new).astype(BF16)
            alpha = jnp.exp2(m_old - m_new)
            acc_ref[g] = alpha * acc_ref[g] + jnp.dot(p, vc, preferred_element_type=F32)
            m_ref[g] = m_new
        return carry

    lax.fori_loop(0, nk, step, 0)
    for g in range(group):
        acc = acc_ref[g]
        o_ref[:, g * dh:(g + 1) * dh] = (acc[:, :dh] / acc[:, dh:dh + 1]).astype(BF16)


def _attn_call(q, k, v, tq, tk):
    bn, lq, dq = q.shape
    lk = k.shape[1]
    dh = HEAD_DIM
    nkv = k.shape[2] // dh
    group = dq // dh // nkv
    assert lq % tq == 0 and lk % tk == 0
    qspec = pl.BlockSpec((None, tq, group * dh), lambda b, h, i: (b, i, h))
    kvspec = pl.BlockSpec((None, lk, dh), lambda b, h, i: (b, 0, h))
    return pl.pallas_call(
        functools.partial(_attn_kernel, group, tk),
        grid=(bn, nkv, lq // tq),
        in_specs=[qspec, kvspec, kvspec],
        out_specs=qspec,
        out_shape=jax.ShapeDtypeStruct((bn, lq, dq), BF16),
        scratch_shapes=[pltpu.VMEM((group, tq, 2 * dh), F32), pltpu.VMEM((group, tq, 1), F32)],
        compiler_params=_params(("parallel", "parallel", "parallel")),
        name="gqa_attn",
    )(q, k, v)


def _attn_fast_kernel(group, nseg, bound_ref, q_ref, *refs):
    o_ref, qt_ref = refs[2 * nseg:]
    dh = HEAD_DIM
    blocks = [(refs[2 * i], refs[2 * i + 1], j) for i in range(nseg)
              for j in range(refs[2 * i].shape[0] // ATT_SUB)]
    nsub = len(blocks)
    ntile = q_ref.shape[0] // ATT_TQ
    for t in range(ntile):
        for g in range(group):
            qt_ref[t * group + g] = (q_ref[t * ATT_TQ:(t + 1) * ATT_TQ, g * dh:(g + 1) * dh]
                                     .astype(F32).T.astype(BF16))
    ones = jnp.ones((ONES_ROWS, ATT_SUB), BF16)
    bound = bound_ref[0]
    work = [(t, s, g) for t in range(ntile) for s in range(nsub) for g in range(group)]

    def scores(t, s, g):
        k_ref, _, j = blocks[s]
        kc = k_ref[j * ATT_SUB:(j + 1) * ATT_SUB, :]
        return jnp.dot(kc, qt_ref[t * group + g], preferred_element_type=F32)

    pending = [scores(*w) for w in work[:ATT_AHEAD]]
    tot = {}
    for idx, (t, s, g) in enumerate(work):
        if idx + ATT_AHEAD < len(work):
            pending.append(scores(*work[idx + ATT_AHEAD]))
        pt = jnp.exp2(pending.pop(0) - bound).astype(BF16)
        _, vt_ref, j = blocks[s]
        vext = jnp.concatenate([vt_ref[:, j * ATT_SUB:(j + 1) * ATT_SUB], ones], axis=0)
        pv = jnp.dot(vext, pt, preferred_element_type=F32)
        tot[t, g] = pv if s == 0 else tot[t, g] + pv
        if s == nsub - 1:
            acc = tot.pop((t, g))
            o_ref[t * ATT_TQ:(t + 1) * ATT_TQ, g * dh:(g + 1) * dh] = (acc[:dh] / acc[dh:dh + 1]).T.astype(BF16)


def _attn_fast_call(bound, q, segments, tq):
    bn, lq, dq = q.shape
    dh = HEAD_DIM
    nkv = segments[0][0].shape[2] // dh
    group = dq // dh // nkv
    assert lq % tq == 0 and tq % ATT_TQ == 0
    qspec = pl.BlockSpec((None, tq, group * dh), lambda b, h, i: (b, i, h))
    in_specs = [pl.BlockSpec(memory_space=pltpu.SMEM), qspec]
    args = [bound, q]
    for k, vt in segments:
        lk = k.shape[1]
        assert lk % ATT_SUB == 0
        in_specs += [pl.BlockSpec((None, lk, dh), lambda b, h, i: (b, 0, h)),
                     pl.BlockSpec((None, dh, lk), lambda b, h, i: (b, h, 0))]
        args += [k, vt]
    return pl.pallas_call(
        functools.partial(_attn_fast_kernel, group, len(segments)),
        grid=(bn, nkv, lq // tq),
        in_specs=in_specs,
        out_specs=qspec,
        out_shape=jax.ShapeDtypeStruct((bn, lq, dq), BF16),
        scratch_shapes=[pltpu.VMEM((tq // ATT_TQ * group, dh, ATT_TQ), BF16)],
        compiler_params=_params(("parallel", "parallel", "parallel")),
        name="gqa_attn_fast",
    )(*args)


def _score_bound(qg, kg):
    return (1.02 * HEAD_DIM * QSCALE) * jnp.max(jnp.abs(qg)) * jnp.max(jnp.abs(kg))


def _attention(q, segments, qg, kg):
    bound = _score_bound(qg, kg).reshape(1)
    tq = ATT_FAST_ROWS if q.shape[1] % ATT_FAST_ROWS == 0 else ATT_TQ

    def online():
        k = jnp.concatenate([k for k, _ in segments], axis=1)
        v = jnp.concatenate([jnp.swapaxes(vt, 1, 2) for _, vt in segments], axis=1)
        return _attn_call(q, k, v, 256, _kv_chunk(k.shape[1]))

    return lax.cond(bound[0] <= SAFE_BOUND, lambda: _attn_fast_call(bound, q, segments, tq), online)


def _kv_chunk(lk):
    for tk in (768, 512, 384, 256, 128):
        if lk % tk == 0:
            return tk
    raise ValueError(f"key length {lk} is not a multiple of 128")


def _mix1_ffn_kernel(o_ref, wm_ref, x_ref, g1_ref, ng1_ref, ng2_ref, sc2_ref, sh2_ref, wi_ref, wo_ref,
                     g2_ref, ng3_ref, out_ref, t_ref, x1_ref):
    _mixer_tail(o_ref[...], wm_ref, x_ref, g1_ref, ng1_ref, x1_ref)
    _ffn_kernel(FFN_COLS, x1_ref, ng2_ref, sc2_ref, sh2_ref, wi_ref, wo_ref, g2_ref, ng3_ref, out_ref, t_ref)


def _mix1_ffn_call(o, w_mix, x, g1, ng1, ng2, sc2, sh2, w_in, w_out, layer, g2, ng3):
    bn, ln, d = x.shape
    tm = _row_tile(ln)
    return pl.pallas_call(
        _mix1_ffn_kernel,
        grid=(bn, ln // tm),
        in_specs=[_rows(tm, o.shape[2]), _full(w_mix.shape), _rows(tm, d), _perb(d), _full((1, d))]
        + _ffn_specs(d, w_in, w_out, layer),
        out_specs=_rows(tm, d),
        out_shape=jax.ShapeDtypeStruct((bn, ln, d), F32),
        scratch_shapes=[pltpu.VMEM((tm, w_out.shape[1]), BF16), pltpu.VMEM((tm, d), F32)],
        compiler_params=_params(("parallel", "parallel")),
        name="mix1_ffn",
    )(o, w_mix, x, g1, ng1, ng2, sc2, sh2, w_in, w_out, g2, ng3)


def kernel(x, c, ctx, c_ctx, ada_w, ada_b, norm_g, ab_w_in, ab_w_out, pool_w, pool_scale, hg_lower, hg_onorm_g,
           att_w_in, att_w_out, att_qnorm_g, att_knorm_g, ffn_w_in, ffn_w_out):
    bn, ln, d = x.shape
    depth = ada_w.shape[0]
    hw = hg_lower.shape[2]
    pwid = pool_scale.shape[1]
    nh = hw // HEAD_DIM

    pad = (-(bn + 1)) % 8
    cc = jnp.concatenate([c, c_ctx[None, :], jnp.zeros((pad, d), F32)], axis=0)
    mods = _ada_call(cc, ada_w, ada_b)

    def split_mods(l, ctx_rows):
        m = mods[l, bn:bn + 1] if ctx_rows else mods[l, :bn]
        m = jnp.broadcast_to(m, (bn, 6 * d)).reshape(bn, 1, 6, d)
        return [m[:, :, t, :] for t in range(6)]

    w_ffn_in = ffn_w_in.astype(BF16)
    w_ffn_out = ffn_w_out.astype(BF16)

    lat, cx = x, ctx
    for l in range(depth):
        j = l // 2
        need_ctx = l < depth - 1
        ng = [norm_g[l, t][None, :] for t in range(4)]
        m_lat = split_mods(l, False)
        m_ctx = split_mods(l, True)

        if l % 2 == 0:
            w_in = ab_w_in[j].astype(BF16)
            w_out = ab_w_out[j].astype(BF16)
            pw = pool_w[j].astype(BF16)
            ps = pool_scale[j][None, :]
            og = hg_onorm_g[j][None, :]
            hgl = hg_lower.reshape(hg_lower.shape[0], 2 * hw)
            pc = _proj0_call(cx, m_ctx[1], m_ctx[0], ng[0], w_in, hgl, j, pwid, hw)
            plat = _proj0_call(lat, m_lat[1], m_lat[0], ng[0], w_in, hgl, j, pwid, hw)
            zero = jnp.zeros((bn, nh, HEAD_DIM, HEAD_DIM), F32)
            u_c, q_c, kf_c, kb_c, lff_c, lfb_c, v_c, g_c = pc
            u_l, q_l, kf_l, kb_l, lff_l, lfb_l, v_l, g_l = plat

            def scan(chunk, direct, tb):
                oc_f, oc_b, s_f, s_b = _hgrn_call(q_c, kf_c, kb_c, lff_c, lfb_c, v_c, zero, zero, chunk, direct, tb)
                ol_f, ol_b, _, _ = _hgrn_call(q_l, kf_l, kb_l, lff_l, lfb_l, v_l, s_f, s_b, chunk, direct, tb)
                return oc_f, oc_b, ol_f, ol_b

            lbound = jnp.cumsum(jax.nn.softmax(hg_lower.astype(F32), axis=0), axis=0)[j]
            worst = HG_CHUNK * jnp.max(-jnp.log(lbound))
            of_c, ob_c, of_l, ob_l = lax.cond(worst <= HG_SAFE_DECAY,
                                              lambda: scan(HG_CHUNK, False, 256),
                                              lambda: scan(HG_SAFE_CHUNK, True, 128))
            lat = _mix0_ffn_call(u_l, of_l, ob_l, g_l, pw, ps, og, w_out, lat, m_lat[2], ng[1], ng[2],
                                 m_lat[4], m_lat[3], w_ffn_in, w_ffn_out, l, m_lat[5], ng[3])
            if need_ctx:
                cx = _mix0_ffn_call(u_c, of_c, ob_c, g_c, pw, ps, og, w_out, cx, m_ctx[2], ng[1], ng[2],
                                    m_ctx[4], m_ctx[3], w_ffn_in, w_ffn_out, l, m_ctx[5], ng[3])
        else:
            perm = _head_perm()
            n_qk = att_w_in.shape[2] // HEAD_DIM - ATT_KV_HEADS
            cols = np.concatenate([hd * HEAD_DIM + perm for hd in range(n_qk)]
                                  + [np.arange(n_qk * HEAD_DIM, att_w_in.shape[2])])
            w_in = att_w_in[j][:, cols].astype(BF16)
            w_out = att_w_out[j].astype(BF16)
            qg = att_qnorm_g[j][perm][None, :]
            kg = att_knorm_g[j][perm][None, :]
            outs_c = _proj1_call(cx, m_ctx[1], m_ctx[0], ng[0], w_in, qg, kg, need_ctx, False)
            q_l, k_l, vt_l = _proj1_call(lat, m_lat[1], m_lat[0], ng[0], w_in, qg, kg, True, True)
            k_c, vt_c = outs_c[-2], outs_c[-1]
            o_l = _attention(q_l, [(k_l, vt_l), (k_c, vt_c)], qg, kg)
            lat = _mix1_ffn_call(o_l, w_out, lat, m_lat[2], ng[1], ng[2], m_lat[4], m_lat[3],
                                 w_ffn_in, w_ffn_out, l, m_lat[5], ng[3])
            if need_ctx:
                o_c = _attn_call(outs_c[0], k_c, jnp.swapaxes(vt_c, 1, 2), 256, _kv_chunk(k_c.shape[1]))
                cx = _mix1_ffn_call(o_c, w_out, cx, m_ctx[2], ng[1], ng[2], m_ctx[4], m_ctx[3],
                                    w_ffn_in, w_ffn_out, l, m_ctx[5], ng[3])
    return lat
```
